```python
import math, functools
import jax, jax.numpy as jnp
from jax import lax
import numpy as np

D_MODEL = 2048
BATCH = 8
SEQ = 4096
DEPTH = 4

GRID_W = 64
CTX_LEN = 256
HEAD_DIM = 128
ATT_HEADS = D_MODEL // 256
ATT_KV_HEADS = ATT_HEADS // 4
ATT_GROUPS = ATT_HEADS // ATT_KV_HEADS
ATT_W = ATT_HEADS * HEAD_DIM
ATT_KV_W = ATT_KV_HEADS * HEAD_DIM
Q_BLOCK = 128
DN_HEADS = D_MODEL // 256
DN_W = DN_HEADS * HEAD_DIM
DN_CONV = 5
DN_CHUNK = 64
RET_HEADS = D_MODEL // 256
RET_W = RET_HEADS * HEAD_DIM
RET_CHUNK = 64
N_BRANCH = 3
BRANCH_W = ATT_W
ROPE_THETA = 10000.0
EPS = 1e-6

_IN_NAMES = ("att_q", "att_k", "att_v", "att_z", "dn_qkv", "dn_z", "dn_b", "dn_a",
             "ret_q", "ret_k", "ret_v", "ret_z", "gates")
_IN_SIZES = (ATT_W, ATT_KV_W, ATT_KV_W, ATT_W, 3 * DN_W, DN_W, 2 * DN_HEADS, 2 * DN_HEADS,
             RET_W, RET_W, RET_W, RET_W, N_BRANCH * D_MODEL)
IN_W = 2 * ATT_W + 2 * ATT_KV_W + 4 * DN_W + 4 * DN_HEADS + 4 * RET_W + N_BRANCH * D_MODEL

kernel_name = "hybrid_gqa_deltanet_retention_dit"


def _rmsnorm(x, g):
    xf = x.astype(jnp.float32)
    y = xf * lax.rsqrt(jnp.mean(xf * xf, axis=-1, keepdims=True) + EPS)
    return (y * g.astype(jnp.float32)).astype(x.dtype)


def _l2norm(x):
    return x * lax.rsqrt(jnp.sum(x * x, axis=-1, keepdims=True) + EPS)


def _axial_rope_tables(rows):
    row, col = jnp.meshgrid(jnp.arange(rows, dtype=jnp.float32),
                            jnp.arange(GRID_W, dtype=jnp.float32), indexing="ij")
    n_freq = HEAD_DIM // 4
    inv = ROPE_THETA ** (-jnp.arange(n_freq, dtype=jnp.float32) / n_freq)
    ang = jnp.concatenate([row.reshape(-1, 1) * inv, col.reshape(-1, 1) * inv], axis=-1)
    return jnp.cos(ang), jnp.sin(ang)


def _rope(x, cos, sin):
    xf = x.astype(jnp.float32).reshape(*x.shape[:-1], HEAD_DIM // 2, 2)
    x0, x1 = xf[..., 0], xf[..., 1]
    c = cos[None, :, None, :]
    s = sin[None, :, None, :]
    out = jnp.stack([x0 * c - x1 * s, x0 * s + x1 * c], axis=-1)
    return out.reshape(x.shape).astype(x.dtype)


def _retention_log_gamma():
    decay = jnp.exp(jnp.linspace(math.log(1.0 / 32), math.log(1.0 / 512), RET_HEADS, dtype=jnp.float32))
    return jnp.log1p(-decay)


def _split_proj(p):
    offsets = [int(o) for o in np.cumsum(np.array(_IN_SIZES))[:-1]]
    return dict(zip(_IN_NAMES, jnp.split(p, offsets, axis=-1)))


def _depthwise_conv(x, w):
    k, ch = w.shape
    return lax.conv_general_dilated(x, w[:, None, :], window_strides=(1,),
                                    padding=[(k // 2, k // 2)],
                                    dimension_numbers=("NWC", "WIO", "NWC"),
                                    feature_group_count=ch)


def _attend(q, k, v):
    s = jnp.einsum("bqhgd,bthd->bhgqt", q, k).astype(jnp.float32) * (HEAD_DIM ** -0.5)
    p = jax.nn.softmax(s, axis=-1).astype(v.dtype)
    return jnp.einsum("bhgqt,bthd->bqhgd", p, v)


def _latent_attention(q, k_all, v_all):
    b, t = q.shape[:2]
    nb = t // Q_BLOCK
    qb = q.reshape(b, nb, Q_BLOCK, ATT_KV_HEADS, ATT_GROUPS, HEAD_DIM).transpose(1, 0, 2, 3, 4, 5)
    ob = lax.map(lambda blk: _attend(blk, k_all, v_all), qb)
    return ob.transpose(1, 0, 2, 3, 4, 5).reshape(b, t, ATT_W)


def _attention_branch(pl, pc, qn_g, kn_g, cos, sin, update_ctx):
    def heads(a, n):
        return a.reshape(*a.shape[:2], n, HEAD_DIM)
    b, t = pl["att_q"].shape[:2]
    q_l = _rope(_rmsnorm(heads(pl["att_q"], ATT_HEADS), qn_g), cos, sin)
    k_l = _rope(_rmsnorm(heads(pl["att_k"], ATT_KV_HEADS), kn_g), cos, sin)
    v_l = heads(pl["att_v"], ATT_KV_HEADS)
    k_c = _rmsnorm(heads(pc["att_k"], ATT_KV_HEADS), kn_g)
    v_c = heads(pc["att_v"], ATT_KV_HEADS)
    k_all = jnp.concatenate([k_c, k_l], axis=1)
    v_all = jnp.concatenate([v_c, v_l], axis=1)
    y_l = _latent_attention(q_l.reshape(b, t, ATT_KV_HEADS, ATT_GROUPS, HEAD_DIM), k_all, v_all)
    y_l = y_l * jax.nn.silu(pl["att_z"])
    y_c = None
    if update_ctx:
        c = pc["att_q"].shape[1]
        q_c = _rmsnorm(heads(pc["att_q"], ATT_HEADS), qn_g)
        y_c = _attend(q_c.reshape(b, c, ATT_KV_HEADS, ATT_GROUPS, HEAD_DIM), k_c, v_c).reshape(b, c, ATT_W)
        y_c = y_c * jax.nn.silu(pc["att_z"])
    return y_l, y_c


def _two_pass(scan_fn, ctx_seq, lat_seq, state0):
    o_c, s_c = scan_fn(*ctx_seq, state0)
    o_l, _ = scan_fn(*lat_seq, s_c)
    return o_c, o_l


def _flip_all(seq):
    return tuple(jnp.flip(a, axis=2) for a in seq)


def _gdn_chunked(q, k, v, g, beta, state):
    b, h, t, dh = q.shape
    n = t // DN_CHUNK
    L = DN_CHUNK
    q, k, v = (a.reshape(b, h, n, L, dh) for a in (q, k, v))
    g = g.reshape(b, h, n, L)
    beta = beta.reshape(b, h, n, L)
    gc = jnp.cumsum(g, axis=-1)
    incl = jnp.tril(jnp.ones((L, L), dtype=bool))
    strict = jnp.tril(jnp.ones((L, L), dtype=bool), -1)
    diff = gc[..., :, None] - gc[..., None, :]
    decay = jnp.where(incl, jnp.exp(jnp.where(incl, diff, 0.0)), 0.0)
    kb = k * beta[..., None]
    m = jnp.where(strict, jnp.einsum("bhnid,bhnjd->bhnij", kb, k) * decay, 0.0)
    a_mat = m + jnp.eye(L, dtype=m.dtype)
    rhs = jnp.concatenate([v * beta[..., None], kb * jnp.exp(gc)[..., None]], axis=-1)
    sol = lax.linalg.triangular_solve(a_mat, rhs, left_side=True, lower=True, unit_diagonal=True)
    u, w = sol[..., :dh], sol[..., dh:]
    qk = jnp.where(incl, jnp.einsum("bhnid,bhnjd->bhnij", q, k) * decay, 0.0)
    q_dec = q * jnp.exp(gc)[..., None]
    k_dec = k * jnp.exp(gc[..., -1:] - gc)[..., None]
    g_last = jnp.exp(gc[..., -1])
    xs = tuple(jnp.moveaxis(a, 2, 0) for a in (q_dec, k_dec, u, w, qk, g_last))

    def step(s, inp):
        qd, kd, u_i, w_i, qk_i, gl = inp
        v_new = u_i - jnp.einsum("bhld,bhde->bhle", w_i, s)
        o = jnp.einsum("bhld,bhde->bhle", qd, s) + jnp.einsum("bhlm,bhme->bhle", qk_i, v_new)
        s = s * gl[..., None, None] + jnp.einsum("bhld,bhle->bhde", kd, v_new)
        return s, o

    state, o = lax.scan(step, state, xs)
    return jnp.moveaxis(o, 0, 2).reshape(b, h, t, dh), state


def _retention_chunked(q, k, v, state, log_gamma):
    b, h, t, dh = q.shape
    L = RET_CHUNK
    n = t // L
    idx = jnp.arange(L, dtype=jnp.float32)
    diff = idx[:, None] - idx[None, :]
    lg = log_gamma[:, None, None]
    dmat = jnp.where(diff >= 0, jnp.exp(jnp.maximum(diff, 0.0) * lg), 0.0)
    xi = jnp.exp((idx + 1.0) * log_gamma[:, None])
    zeta = jnp.exp((L - 1.0 - idx) * log_gamma[:, None])
    g_chunk = jnp.exp(L * log_gamma)
    q, k, v = (a.reshape(b, h, n, L, dh) for a in (q, k, v))
    intra = jnp.einsum("bhnlm,bhnme->bhnle",
                       jnp.einsum("bhnld,bhnmd->bhnlm", q, k) * dmat[None, :, None], v)
    q_x = q * xi[None, :, None, :, None]
    k_z = k * zeta[None, :, None, :, None]
    xs = tuple(jnp.moveaxis(a, 2, 0) for a in (q_x, k_z, v, intra))

    def step(s, inp):
        qx, kz, v_i, intra_i = inp
        o = intra_i + jnp.einsum("bhld,bhde->bhle", qx, s)
        s = s * g_chunk[None, :, None, None] + jnp.einsum("bhld,bhle->bhde", kz, v_i)
        return s, o

    state, o = lax.scan(step, state, xs)
    return jnp.moveaxis(o, 0, 2).reshape(b, h, t, dh), state


def _dn_inputs(p, conv_w, a_log, dt_bias):
    b, t = p["dn_qkv"].shape[:2]
    qkv = jax.nn.silu(_depthwise_conv(p["dn_qkv"].astype(jnp.float32), conv_w.astype(jnp.float32)))
    q, k, v = jnp.split(qkv, 3, axis=-1)
    heads = lambda a: a.reshape(b, t, DN_HEADS, HEAD_DIM).transpose(0, 2, 1, 3)
    q = _l2norm(heads(q)) * (HEAD_DIM ** -0.5)
    k = _l2norm(heads(k))
    v = heads(v)
    per_dir = lambda a: a.astype(jnp.float32).reshape(b, t, 2, DN_HEADS).transpose(2, 0, 3, 1)
    beta = jax.nn.sigmoid(per_dir(p["dn_b"]))
    g = -jnp.exp(a_log.astype(jnp.float32))[:, None, :, None] * jax.nn.softplus(
        per_dir(p["dn_a"]) + dt_bias.astype(jnp.float32)[:, None, :, None])
    return q, k, v, g, beta


def _dn_out(o, g, z):
    o = o.transpose(0, 2, 1, 3)
    y = _rmsnorm(o, g).reshape(*o.shape[:2], DN_W)
    return y.astype(z.dtype) * jax.nn.silu(z)


def _deltanet_branch(pl, pc, conv_w, a_log, dt_bias, on_g, update_ctx):
    ql, kl, vl, gl, bl = _dn_inputs(pl, conv_w, a_log, dt_bias)
    qc, kc, vc, gc, bc = _dn_inputs(pc, conv_w, a_log, dt_bias)
    zero = jnp.zeros((ql.shape[0], DN_HEADS, HEAD_DIM, HEAD_DIM), jnp.float32)
    oc_f, ol_f = _two_pass(_gdn_chunked, (qc, kc, vc, gc[0], bc[0]), (ql, kl, vl, gl[0], bl[0]), zero)
    oc_b, ol_b = _two_pass(_gdn_chunked, _flip_all((qc, kc, vc, gc[1], bc[1])),
                           _flip_all((ql, kl, vl, gl[1], bl[1])), zero)
    y_l = _dn_out(ol_f + jnp.flip(ol_b, axis=2), on_g, pl["dn_z"])
    y_c = _dn_out(oc_f + jnp.flip(oc_b, axis=2), on_g, pc["dn_z"]) if update_ctx else None
    return y_l, y_c


def _ret_out(o, g, z):
    o = o.transpose(0, 2, 1, 3)
    mu = jnp.mean(o, axis=-1, keepdims=True)
    var = jnp.mean(jnp.square(o - mu), axis=-1, keepdims=True)
    y = ((o - mu) * lax.rsqrt(var + EPS)).reshape(*o.shape[:2], RET_W) * g.astype(jnp.float32)
    return y.astype(z.dtype) * jax.nn.silu(z)


def _retention_branch(pl, pc, gn_g, cos, sin, log_gamma, update_ctx):
    scale = HEAD_DIM ** -0.5

    def heads(a, rope):
        h = a.reshape(*a.shape[:2], RET_HEADS, HEAD_DIM)
        if rope:
            h = _rope(h, cos, sin)
        return h.astype(jnp.float32).transpose(0, 2, 1, 3)

    ql, kl, vl = heads(pl["ret_q"], True), heads(pl["ret_k"], True) * scale, heads(pl["ret_v"], False)
    qc, kc, vc = heads(pc["ret_q"], False), heads(pc["ret_k"], False) * scale, heads(pc["ret_v"], False)
    zero = jnp.zeros((ql.shape[0], RET_HEADS, HEAD_DIM, HEAD_DIM), jnp.float32)
    scan_f = functools.partial(_retention_chunked, log_gamma=log_gamma)
    scan_b = functools.partial(_retention_chunked, log_gamma=log_gamma[::-1])
    oc_f, ol_f = _two_pass(scan_f, (qc, kc, vc), (ql, kl, vl), zero)
    oc_b, ol_b = _two_pass(scan_b, _flip_all((qc, kc, vc)), _flip_all((ql, kl, vl)), zero)
    y_l = _ret_out(ol_f + jnp.flip(ol_b, axis=2), gn_g, pl["ret_z"])
    y_c = _ret_out(oc_f + jnp.flip(oc_b, axis=2), gn_g, pc["ret_z"]) if update_ctx else None
    return y_l, y_c


def _merge(ys, gate_logits, w_branch, w_out):
    gates = jax.nn.sigmoid(gate_logits.astype(jnp.float32)).astype(gate_logits.dtype)
    gates = jnp.split(gates, N_BRANCH, axis=-1)
    merged = gates[0] * (ys[0] @ w_branch[0])
    for i in range(1, N_BRANCH):
        merged = merged + gates[i] * (ys[i] @ w_branch[i])
    return merged @ w_out


def _layer(x, ctx, c, c_ctx, norm_g, w_mod, b_mod, w_in, att_qnorm_g, att_knorm_g,
           dn_conv_w, dn_a_log, dn_dt_bias, dn_onorm_g, ret_gnorm_g, w_branch, w_out,
           cos, sin, log_gamma, update_ctx):
    mod_lat = jax.nn.silu(c) @ w_mod + b_mod
    mod_ctx = jax.nn.silu(c_ctx) @ w_mod + b_mod
    shift_l, scale_l, gate_l = jnp.split(mod_lat, 3, axis=-1)
    shift_c, scale_c, gate_c = jnp.split(mod_ctx, 3, axis=-1)
    h_lat = _rmsnorm(x, norm_g) * (1 + scale_l[:, None]) + shift_l[:, None]
    h_ctx = _rmsnorm(ctx, norm_g) * (1 + scale_c) + shift_c
    pl = _split_proj(h_lat @ w_in)
    pc = _split_proj(h_ctx @ w_in)
    ya_l, ya_c = _attention_branch(pl, pc, att_qnorm_g, att_knorm_g, cos, sin, update_ctx)
    yd_l, yd_c = _deltanet_branch(pl, pc, dn_conv_w, dn_a_log, dn_dt_bias, dn_onorm_g, update_ctx)
    yr_l, yr_c = _retention_branch(pl, pc, ret_gnorm_g, cos, sin, log_gamma, update_ctx)
    x = x + gate_l[:, None] * _merge((ya_l, yd_l, yr_l), pl["gates"], w_branch, w_out)
    if update_ctx:
        ctx = ctx + gate_c * _merge((ya_c, yd_c, yr_c), pc["gates"], w_branch, w_out)
    return x, ctx


def setup_inputs(seed: int = 0) -> dict:
    key = jax.random.key(seed)
    ks = jax.random.split(key, 17)
    f32 = jnp.float32
    nrm = lambda k, shape, s: jax.random.normal(k, shape, f32) * s
    dt = jnp.exp(jax.random.uniform(ks[11], (DEPTH, 2, DN_HEADS), f32,
                                    math.log(1e-3), math.log(1e-1)))
    return {
        "x": nrm(ks[0], (BATCH, SEQ, D_MODEL), 1.0),
        "c": nrm(ks[1], (BATCH, D_MODEL), 1.0),
        "ctx": nrm(ks[2], (BATCH, CTX_LEN, D_MODEL), 1.0),
        "c_ctx": nrm(ks[3], (D_MODEL,), 1.0),
        "norm_g": 1.0 + nrm(ks[4], (DEPTH, D_MODEL), 0.02),
        "w_mod": nrm(ks[5], (DEPTH, D_MODEL, 3 * D_MODEL), 0.5 * D_MODEL ** -0.5),
        "b_mod": nrm(ks[6], (DEPTH, 3 * D_MODEL), 0.01),
        "w_in": nrm(ks[7], (DEPTH, D_MODEL, IN_W), D_MODEL ** -0.5),
        "att_qnorm_g": 1.0 + nrm(ks[8], (DEPTH, HEAD_DIM), 0.02),
        "att_knorm_g": 1.0 + nrm(ks[9], (DEPTH, HEAD_DIM), 0.02),
        "dn_conv_w": nrm(ks[10], (DEPTH, DN_CONV, 3 * DN_W), DN_CONV ** -0.5),
        "dn_a_log": jnp.log(jax.random.uniform(ks[12], (DEPTH, 2, DN_HEADS), f32, 1.0, 16.0)),
        "dn_dt_bias": dt + jnp.log(-jnp.expm1(-dt)),
        "dn_onorm_g": 1.0 + nrm(ks[13], (DEPTH, HEAD_DIM), 0.02),
        "ret_gnorm_g": 1.0 + nrm(ks[14], (DEPTH, RET_W), 0.02),
        "w_branch": nrm(ks[15], (DEPTH, N_BRANCH, BRANCH_W, D_MODEL), BRANCH_W ** -0.5),
        "w_out": nrm(ks[16], (DEPTH, D_MODEL, D_MODEL), D_MODEL ** -0.5),
    }


def reference(x, c, ctx, c_ctx, norm_g, w_mod, b_mod, w_in, att_qnorm_g, att_knorm_g,
              dn_conv_w, dn_a_log, dn_dt_bias, dn_onorm_g, ret_gnorm_g, w_branch, w_out):
    ROWS = x.shape[1] // GRID_W
    cos, sin = _axial_rope_tables(ROWS)
    log_gamma = _retention_log_gamma()
    for l in range(DEPTH):
        x, ctx = _layer(x, ctx, c, c_ctx, norm_g[l], w_mod[l], b_mod[l], w_in[l],
                        att_qnorm_g[l], att_knorm_g[l], dn_conv_w[l], dn_a_log[l], dn_dt_bias[l],
                        dn_onorm_g[l], ret_gnorm_g[l], w_branch[l], w_out[l],
                        cos, sin, log_gamma, update_ctx=(l < DEPTH - 1))
    return x
```

```python
import functools
import math

import numpy as np
import jax
import jax.numpy as jnp
from jax import lax
from jax.experimental import pallas as pl
from jax.experimental.pallas import tpu as pltpu

F32 = jnp.float32
BF16 = jnp.bfloat16

HEAD_DIM = 128
GRID_W = 64
ATT_GROUPS = 4
DN_CONV = 5
CHUNK = 64
ROPE_THETA = 10000.0
EPS = 1e-6
N_BRANCH = 3
V7X_VMEM_LIMIT = 56 * 1024 * 1024


def _sigmoid(x):
    return 1.0 / (1.0 + jnp.exp(-x))


def _silu(x):
    return x * _sigmoid(x)


def _softplus(x):
    return jnp.maximum(x, 0.0) + jnp.log1p(jnp.exp(-jnp.abs(x)))


def _dot(a, b):
    return jnp.dot(a.astype(BF16), b.astype(BF16), preferred_element_type=F32)


def _dot_nt(a, b):
    return lax.dot_general(a.astype(BF16), b.astype(BF16), (((1,), (1,)), ((), ())),
                           preferred_element_type=F32)


def _split2(x):
    hi = x.astype(BF16)
    lo = (x - hi.astype(F32)).astype(BF16)
    return hi, lo


def _split3(x):
    hi = x.astype(BF16)
    r = x - hi.astype(F32)
    mid = r.astype(BF16)
    lo = (r - mid.astype(F32)).astype(BF16)
    return hi, mid, lo


def _dot_hi(a, b):
    ah, al = _split2(a)
    bh, bl = _split2(b)
    d = lambda u, v: jnp.dot(u, v, preferred_element_type=F32)
    return d(ah, bh) + (d(ah, bl) + d(al, bh))


def _dot_sel_l(sel, x):
    hi, mid, lo = _split3(x)
    d = lambda v: jnp.dot(sel, v, preferred_element_type=F32)
    return d(hi) + (d(mid) + d(lo))


def _dot_sel_r(x, sel):
    hi, mid, lo = _split3(x)
    d = lambda v: jnp.dot(v, sel, preferred_element_type=F32)
    return d(hi) + (d(mid) + d(lo))


def _cparams(sem):
    return pltpu.CompilerParams(dimension_semantics=sem, vmem_limit_bytes=V7X_VMEM_LIMIT)


def _mod_kernel(c_ref, w_ref, b_ref, o_ref):
    a = _silu(c_ref[...])
    o_ref[0] = _dot(a, w_ref[0]) + b_ref[0]


def _modulation(cc, w_mod, b_mod, tn=512):
    depth, d, n = w_mod.shape
    rows = cc.shape[0]
    return pl.pallas_call(
        _mod_kernel,
        grid=(depth, n // tn),
        in_specs=[pl.BlockSpec((rows, d), lambda l, j: (0, 0)),
                  pl.BlockSpec((1, d, tn), lambda l, j: (l, 0, j)),
                  pl.BlockSpec((1, 1, tn), lambda l, j: (l, 0, j))],
        out_specs=pl.BlockSpec((1, rows, tn), lambda l, j: (l, 0, j)),
        out_shape=jax.ShapeDtypeStruct((depth, rows, n), F32),
        compiler_params=_cparams(("parallel", "parallel")),
        name="modulation",
    )(cc, w_mod, b_mod.reshape(depth, 1, n))


def _inproj_kernel(x_ref, g_ref, lat_ref, ctx_ref, w_ref, o_ref, h_scr, *, n_ctx, tm, d):
    i = pl.program_id(1)
    j = pl.program_id(2)

    @pl.when(j == 0)
    def _():
        x = x_ref[0]
        y = x * lax.rsqrt(jnp.mean(x * x, axis=-1, keepdims=True) + EPS) * g_ref[...]
        row = i * tm + lax.broadcasted_iota(jnp.int32, (tm, 1), 0)
        is_ctx = row < n_ctx
        shift = jnp.where(is_ctx, ctx_ref[0:1, :], lat_ref[0, 0:1, :])
        scale = jnp.where(is_ctx, ctx_ref[1:2, :], lat_ref[0, 1:2, :])
        h_scr[...] = (y * (1.0 + scale) + shift).astype(BF16)

    o_ref[0] = jnp.dot(h_scr[...], w_ref[...], preferred_element_type=F32).astype(o_ref.dtype)


def _in_projection(xs, norm_g, mod_lat, mod_ctx, w, n_ctx, tm, tn=512):
    b, s, d = xs.shape
    n = w.shape[1]
    kern = functools.partial(_inproj_kernel, n_ctx=n_ctx, tm=tm, d=d)
    return pl.pallas_call(
        kern,
        grid=(b, s // tm, n // tn),
        in_specs=[pl.BlockSpec((1, tm, d), lambda bb, i, j: (bb, i, 0)),
                  pl.BlockSpec((1, d), lambda bb, i, j: (0, 0)),
                  pl.BlockSpec((1, 3, d), lambda bb, i, j: (bb, 0, 0)),
                  pl.BlockSpec((3, d), lambda bb, i, j: (0, 0)),
                  pl.BlockSpec((d, tn), lambda bb, i, j: (0, j))],
        out_specs=pl.BlockSpec((1, tm, tn), lambda bb, i, j: (bb, i, j)),
        out_shape=jax.ShapeDtypeStruct((b, s, n), F32),
        scratch_shapes=[pltpu.VMEM((tm, d), BF16)],
        compiler_params=_cparams(("parallel", "parallel", "arbitrary")),
        name="in_projection",
    )(xs, norm_g.reshape(1, d), mod_lat, mod_ctx, w)


def _attn_kernel(q_ref, k_ref, v_ref, z_ref, y_ref, *, n_ctx):
    i = pl.program_id(2)
    q = q_ref[0]

    def attend(kv_len):
        k = k_ref[0, :kv_len, :]
        v = v_ref[0, :kv_len, :].astype(BF16)
        s = _dot_nt(q, k)
        m = jnp.max(s, axis=-1, keepdims=True)
        p = jnp.exp(s - m)
        l = jnp.sum(p, axis=-1, keepdims=True)
        o = jnp.dot(p.astype(BF16), v, preferred_element_type=F32) / l
        y_ref[0] = (o * _silu(z_ref[0])).astype(y_ref.dtype)

    @pl.when(i == 0)
    def _():
        attend(n_ctx)

    @pl.when(i > 0)
    def _():
        attend(k_ref.shape[1])


def _attention(q, k, p, n_ctx, col_v, col_z):
    b, s, aw = q.shape
    nh = aw // HEAD_DIM
    tq = n_ctx
    kern = functools.partial(_attn_kernel, n_ctx=n_ctx)
    vb, zb = col_v // HEAD_DIM, col_z // HEAD_DIM
    return pl.pallas_call(
        kern,
        grid=(b, nh, s // tq),
        in_specs=[pl.BlockSpec((1, tq, HEAD_DIM), lambda bb, h, i: (bb, i, h)),
                  pl.BlockSpec((1, s, HEAD_DIM), lambda bb, h, i: (bb, 0, h // ATT_GROUPS)),
                  pl.BlockSpec((1, s, HEAD_DIM), lambda bb, h, i: (bb, 0, vb + h // ATT_GROUPS)),
                  pl.BlockSpec((1, tq, HEAD_DIM), lambda bb, h, i: (bb, i, zb + h))],
        out_specs=pl.BlockSpec((1, tq, HEAD_DIM), lambda bb, h, i: (bb, i, h)),
        out_shape=jax.ShapeDtypeStruct((b, s, aw), BF16),
        compiler_params=_cparams(("parallel", "parallel", "arbitrary")),
        name="attention",
    )(q, k, p, p)


def _backward_chunk(n, n_chunks, n_ctx_chunks):
    return jnp.where(n < n_ctx_chunks, n_ctx_chunks - 1 - n, n_chunks + n_ctx_chunks - 1 - n)


def _unit_tri_inverse(m, eye, blk8, c16, c32, c64):
    nm = jnp.where(blk8, -m, 0.0)
    n2 = _dot_hi(nm, nm)
    n4 = _dot_hi(n2, n2)
    t = eye + nm
    t = t + _dot_hi(t, n2)
    t = t + _dot_hi(t, n4)
    for cm in (c16, c32, c64):
        c = jnp.where(cm, m, 0.0)
        t = t - _dot_hi(_dot_hi(t, c), t)
    return t


def _gdn_kernel(nega_ref, dtb_ref, q_ref, k_ref, v_ref, kt_ref, bac_ref, bar_ref, z_ref, g_ref,
                y_ref, o_scr, *, n_chunks, n_ctx_chunks, n_heads):
    h = pl.program_id(1)
    L = CHUNK
    ii = lax.broadcasted_iota(jnp.int32, (L, L), 0)
    jj = lax.broadcasted_iota(jnp.int32, (L, L), 1)
    lane = lax.broadcasted_iota(jnp.int32, (L, HEAD_DIM), 1)
    eye = jnp.where(ii == jj, 1.0, 0.0)
    same8 = (ii >> 3) == (jj >> 3)
    same16 = (ii >> 4) == (jj >> 4)
    same32 = (ii >> 5) == (jj >> 5)
    c16 = same16 & jnp.logical_not(same8)
    c32 = same32 & jnp.logical_not(same16)
    c64 = jnp.logical_not(same32)
    lower = (ii >= jj)
    upper = (ii <= jj)
    lower_b = jnp.where(lower, 1.0, 0.0).astype(BF16)
    upper_b = jnp.where(upper, 1.0, 0.0).astype(BF16)
    incl = (lower, upper)
    strict = (ii > jj, ii < jj)
    cum_col = (lower_b, upper_b)
    cum_row = (upper_b, lower_b)

    o_scr[...] = jnp.zeros_like(o_scr)

    def chunk(c, d, state):
        r0 = pl.multiple_of(c * L, L)
        q = q_ref[0, pl.ds(r0, L), :]
        k = k_ref[0, pl.ds(r0, L), :]
        v = v_ref[0, pl.ds(r0, L), :]
        kt = kt_ref[0, 0, c]
        ba = bac_ref[0, pl.ds(r0, L), :]
        rows = bar_ref[0, 0, c]
        nega = nega_ref[d, h]
        dtb = dtb_ref[d, h]
        col_b = d * n_heads + h
        col_a = (2 + d) * n_heads + h
        b_col = jnp.sum(jnp.where(lane == col_b, ba, 0.0), axis=1, keepdims=True)
        a_col = jnp.sum(jnp.where(lane == col_a, ba, 0.0), axis=1, keepdims=True)
        beta_c = _sigmoid(b_col)
        g_cb = jnp.broadcast_to(nega * _softplus(a_col + dtb), (L, HEAD_DIM))
        p_cb = _dot_sel_l(cum_col[d], g_cb)
        gtot_cb = jnp.sum(g_cb, axis=0, keepdims=True)
        g_r = nega * _softplus(rows[2 + d:3 + d, :] + dtb)
        p_r = _dot_sel_r(jnp.broadcast_to(g_r, (8, L)), cum_row[d])[0:1, :]
        gtot_r = jnp.sum(g_r, axis=1, keepdims=True)

        diff = p_cb[:, :L] - p_r
        decay = jnp.where(incl[d], jnp.exp(jnp.where(incl[d], diff, 0.0)), 0.0)
        a = _dot(jnp.concatenate([k, q], axis=0), kt)
        m = jnp.where(strict[d], beta_c * a[:L] * decay, 0.0)
        qk = a[L:] * decay
        t = _unit_tri_inverse(m, eye, same8, c16, c32, c64)
        e_in = jnp.exp(p_cb)
        rhs = jnp.concatenate([v * beta_c, k * (beta_c * e_in)], axis=1)
        uw = _dot(t, rhs)
        u, w = uw[:, :HEAD_DIM], uw[:, HEAD_DIM:]
        ws = _dot(jnp.concatenate([w, q * e_in], axis=0), state)
        v_new = u - ws[:L]
        o = ws[L:] + _dot(qk, v_new)
        kdt = kt * jnp.exp(gtot_r - p_r)
        state = state * jnp.exp(gtot_cb) + _dot(kdt, v_new)
        o_scr[pl.ds(r0, L), :] += o
        return state

    def body(n, carry):
        sf, sb = carry
        sf = chunk(n, 0, sf)
        sb = chunk(_backward_chunk(n, n_chunks, n_ctx_chunks), 1, sb)
        return sf, sb

    zero = jnp.zeros((HEAD_DIM, HEAD_DIM), F32)
    lax.fori_loop(0, n_chunks, body, (zero, zero))

    def epilogue(n, _):
        r0 = pl.multiple_of(n * L, L)
        o = o_scr[pl.ds(r0, L), :]
        y = o * lax.rsqrt(jnp.mean(o * o, axis=-1, keepdims=True) + EPS) * g_ref[...]
        y_ref[0, pl.ds(r0, L), :] = (y * _silu(z_ref[0, pl.ds(r0, L), :])).astype(y_ref.dtype)
        return 0

    lax.fori_loop(0, n_chunks, epilogue, 0)


def _gdn(nega, dtb, q, k, v, kt, bar, p, onorm_g, n_ctx, col_ba, col_z):
    b, s, w = q.shape
    nh = w // HEAD_DIM
    nc = s // CHUNK
    kern = functools.partial(_gdn_kernel, n_chunks=nc, n_ctx_chunks=n_ctx // CHUNK, n_heads=nh)
    seq = lambda off: pl.BlockSpec((1, s, HEAD_DIM), lambda bb, h: (bb, 0, off + h))
    smem = pl.BlockSpec(memory_space=pltpu.SMEM)
    return pl.pallas_call(
        kern,
        grid=(b, nh),
        in_specs=[smem, smem, seq(0), seq(0), seq(0),
                  pl.BlockSpec((1, 1, nc, HEAD_DIM, CHUNK), lambda bb, h: (bb, h, 0, 0, 0)),
                  pl.BlockSpec((1, s, HEAD_DIM), lambda bb, h: (bb, 0, col_ba // HEAD_DIM)),
                  pl.BlockSpec((1, 1, nc, 8, CHUNK), lambda bb, h: (bb, h, 0, 0, 0)),
                  seq(col_z // HEAD_DIM),
                  pl.BlockSpec((1, HEAD_DIM), lambda bb, h: (0, 0))],
        out_specs=seq(0),
        out_shape=jax.ShapeDtypeStruct((b, s, w), BF16),
        scratch_shapes=[pltpu.VMEM((s, HEAD_DIM), F32)],
        compiler_params=_cparams(("parallel", "arbitrary")),
        name="gated_deltanet",
    )(nega, dtb, q, k, v, kt, p, bar, p, onorm_g.reshape(1, HEAD_DIM))


def _ret_kernel(lg_ref, q_ref, v_ref, kt_ref, z_ref, g_ref, y_ref, o_scr,
                *, n_chunks, n_ctx_chunks):
    h = pl.program_id(1)
    L = CHUNK
    ii = lax.broadcasted_iota(jnp.int32, (L, L), 0)
    jj = lax.broadcasted_iota(jnp.int32, (L, L), 1)
    row_i = lax.broadcasted_iota(jnp.int32, (L, HEAD_DIM), 0).astype(F32)
    col_j = lax.broadcasted_iota(jnp.int32, (HEAD_DIM, L), 1).astype(F32)
    dist = (ii - jj).astype(F32)
    lgs = (lg_ref[0, h], lg_ref[1, h])
    dmat = (jnp.where(ii >= jj, jnp.exp(jnp.maximum(dist, 0.0) * lgs[0]), 0.0),
            jnp.where(ii <= jj, jnp.exp(jnp.maximum(-dist, 0.0) * lgs[1]), 0.0))
    xi = (jnp.exp((row_i + 1.0) * lgs[0]), jnp.exp((L - row_i) * lgs[1]))
    zeta = (jnp.exp((L - 1.0 - col_j) * lgs[0]), jnp.exp(col_j * lgs[1]))
    gch = tuple(jnp.exp(jnp.full((1, HEAD_DIM), float(L), F32) * lg) for lg in lgs)

    o_scr[...] = jnp.zeros_like(o_scr)

    def chunk(c, d, state):
        r0 = pl.multiple_of(c * L, L)
        q = q_ref[0, pl.ds(r0, L), :]
        v = v_ref[0, pl.ds(r0, L), :]
        kt = kt_ref[0, 0, c]
        p = _dot(q, kt) * dmat[d]
        o = _dot(p, v) + xi[d] * _dot(q, state)
        state = state * gch[d] + _dot(kt * zeta[d], v)
        o_scr[pl.ds(r0, L), :] += o
        return state

    def body(n, carry):
        sf, sb = carry
        sf = chunk(n, 0, sf)
        sb = chunk(_backward_chunk(n, n_chunks, n_ctx_chunks), 1, sb)
        return sf, sb

    zero = jnp.zeros((HEAD_DIM, HEAD_DIM), F32)
    lax.fori_loop(0, n_chunks, body, (zero, zero))

    def epilogue(n, _):
        r0 = pl.multiple_of(n * L, L)
        o = o_scr[pl.ds(r0, L), :]
        mu = jnp.mean(o, axis=-1, keepdims=True)
        var = jnp.mean(jnp.square(o - mu), axis=-1, keepdims=True)
        y = (o - mu) * lax.rsqrt(var + EPS) * g_ref[...]
        y_ref[0, pl.ds(r0, L), :] = (y * _silu(z_ref[0, pl.ds(r0, L), :])).astype(y_ref.dtype)
        return 0

    lax.fori_loop(0, n_chunks, epilogue, 0)


def _retention(lg, q, kt, p, gnorm_g, n_ctx, col_v, col_z):
    b, s, w = q.shape
    nh = w // HEAD_DIM
    nc = s // CHUNK
    kern = functools.partial(_ret_kernel, n_chunks=nc, n_ctx_chunks=n_ctx // CHUNK)
    seq = lambda off: pl.BlockSpec((1, s, HEAD_DIM), lambda bb, h: (bb, 0, off + h))
    return pl.pallas_call(
        kern,
        grid=(b, nh),
        in_specs=[pl.BlockSpec(memory_space=pltpu.SMEM), seq(0), seq(col_v // HEAD_DIM),
                  pl.BlockSpec((1, 1, nc, HEAD_DIM, CHUNK), lambda bb, h: (bb, h, 0, 0, 0)),
                  seq(col_z // HEAD_DIM),
                  pl.BlockSpec((1, HEAD_DIM), lambda bb, h: (0, h))],
        out_specs=seq(0),
        out_shape=jax.ShapeDtypeStruct((b, s, w), BF16),
        scratch_shapes=[pltpu.VMEM((s, HEAD_DIM), F32)],
        compiler_params=_cparams(("parallel", "arbitrary")),
        name="retention",
    )(lg, q, p, kt, p, gnorm_g.reshape(1, w))


def _merge_kernel(ya_ref, yd_ref, yr_ref, wa_ref, wd_ref, wr_ref, ga_ref, gd_ref, gr_ref, o_ref):
    acc = _sigmoid(ga_ref[0]) * jnp.dot(ya_ref[0], wa_ref[0], preferred_element_type=F32)
    acc = acc + _sigmoid(gd_ref[0]) * jnp.dot(yd_ref[0], wd_ref[0], preferred_element_type=F32)
    acc = acc + _sigmoid(gr_ref[0]) * jnp.dot(yr_ref[0], wr_ref[0], preferred_element_type=F32)
    o_ref[0] = acc.astype(o_ref.dtype)


def _merge(ya, yd, yr, w_branch, p, col_gates, tm, tn=512):
    b, s, bw = ya.shape
    d = w_branch.shape[2]
    gb = col_gates // tn
    ysp = pl.BlockSpec((1, tm, bw), lambda bb, i, j: (bb, i, 0))
    wsp = lambda br: pl.BlockSpec((1, bw, tn), lambda bb, i, j: (br, 0, j))
    gsp = lambda br: pl.BlockSpec((1, tm, tn), lambda bb, i, j: (bb, i, gb + br * (d // tn) + j))
    return pl.pallas_call(
        _merge_kernel,
        grid=(b, s // tm, d // tn),
        in_specs=[ysp, ysp, ysp, wsp(0), wsp(1), wsp(2), gsp(0), gsp(1), gsp(2)],
        out_specs=pl.BlockSpec((1, tm, tn), lambda bb, i, j: (bb, i, j)),
        out_shape=jax.ShapeDtypeStruct((b, s, d), BF16),
        compiler_params=_cparams(("parallel", "parallel", "arbitrary")),
        name="branch_merge",
    )(ya, yd, yr, w_branch, w_branch, w_branch, p, p, p)


def _outproj_kernel(m_ref, w_ref, x_ref, lat_ref, ctx_ref, o_ref, *, n_ctx, tm):
    i = pl.program_id(1)
    row = i * tm + lax.broadcasted_iota(jnp.int32, (tm, 1), 0)
    gate = jnp.where(row < n_ctx, ctx_ref[...], lat_ref[0])
    o_ref[0] = x_ref[0] + gate * jnp.dot(m_ref[0], w_ref[...], preferred_element_type=F32)


def _out_projection(merged, w_out, xs, gate_lat, gate_ctx, n_ctx, tm, tn=512):
    b, s, d = xs.shape
    kern = functools.partial(_outproj_kernel, n_ctx=n_ctx, tm=tm)
    return pl.pallas_call(
        kern,
        grid=(b, s // tm, d // tn),
        in_specs=[pl.BlockSpec((1, tm, d), lambda bb, i, j: (bb, i, 0)),
                  pl.BlockSpec((d, tn), lambda bb, i, j: (0, j)),
                  pl.BlockSpec((1, tm, tn), lambda bb, i, j: (bb, i, j)),
                  pl.BlockSpec((1, 1, tn), lambda bb, i, j: (bb, 0, j)),
                  pl.BlockSpec((1, tn), lambda bb, i, j: (0, j))],
        out_specs=pl.BlockSpec((1, tm, tn), lambda bb, i, j: (bb, i, j)),
        out_shape=jax.ShapeDtypeStruct((b, s, d), F32),
        compiler_params=_cparams(("parallel", "parallel", "arbitrary")),
        name="out_projection",
    )(merged, w_out, xs, gate_lat, gate_ctx)


def _deinterleave(w, n_heads):
    lead = w.shape[:-1]
    w = w.reshape(*lead, n_heads, HEAD_DIM // 2, 2)
    return jnp.swapaxes(w, -1, -2).reshape(*lead, n_heads * HEAD_DIM)


class _Cols:
    def __init__(self, d_model):
        aw = d_model // 2
        kvw = aw // ATT_GROUPS
        nh = aw // HEAD_DIM
        self.src = dict(att_q=aw, att_k=kvw, att_v=kvw, att_z=aw, dn_qkv=3 * aw, dn_z=aw,
                        dn_b=2 * nh, dn_a=2 * nh, ret_q=aw, ret_k=aw, ret_v=aw, ret_z=aw,
                        gates=N_BRANCH * d_model)
        order = ("att_q", "att_k", "att_v", "att_z", "dn_qkv", "dn_z", "ret_q", "ret_k", "ret_v",
                 "ret_z", "gates", "dn_b", "dn_a")
        self.order = order
        self.off = {}
        o = 0
        for name in order:
            self.off[name] = o
            o += self.src[name]
        self.used = o
        self.n_heads = nh


def _prep_w_in(w_in, cols, tn):
    offs, o = {}, 0
    for name, width in cols.src.items():
        offs[name] = o
        o += width
    parts = []
    for name in cols.order:
        seg = w_in[:, offs[name]:offs[name] + cols.src[name]]
        if name in ("att_q", "att_k", "ret_q", "ret_k"):
            seg = _deinterleave(seg, cols.src[name] // HEAD_DIM)
        parts.append(seg)
    pad = (-cols.used) % tn
    if pad:
        parts.append(jnp.zeros((w_in.shape[0], pad), w_in.dtype))
    return jnp.concatenate(parts, axis=1).astype(BF16)


def _rope_tables(n_ctx, t):
    rows = t // GRID_W
    row, col = jnp.meshgrid(jnp.arange(rows, dtype=F32), jnp.arange(GRID_W, dtype=F32), indexing="ij")
    n_freq = HEAD_DIM // 4
    inv = ROPE_THETA ** (-jnp.arange(n_freq, dtype=F32) / n_freq)
    ang = jnp.concatenate([row.reshape(-1, 1) * inv, col.reshape(-1, 1) * inv], axis=-1)
    cos, sin = jnp.cos(ang), jnp.sin(ang)
    cos = jnp.concatenate([jnp.ones((n_ctx, HEAD_DIM // 2), F32), cos], axis=0)
    sin = jnp.concatenate([jnp.zeros((n_ctx, HEAD_DIM // 2), F32), sin], axis=0)
    return jnp.concatenate([cos, cos], axis=-1), jnp.concatenate([-sin, sin], axis=-1)


def _rope(x, c2, s2):
    return x * c2[None, :, None, :] + jnp.roll(x, HEAD_DIM // 2, axis=-1) * s2[None, :, None, :]


def _chunked_t(k, n_heads):
    b, s, _ = k.shape
    k = k.reshape(b, s // CHUNK, CHUNK, n_heads, HEAD_DIM)
    return jnp.transpose(k, (0, 3, 1, 4, 2))


def _segment_conv(x, w, n_ctx):
    k = w.shape[0]

    def conv(seg):
        n = seg.shape[1]
        pad = jnp.pad(seg, ((0, 0), (k // 2, k // 2), (0, 0)))
        return sum(pad[:, j:j + n, :] * w[j] for j in range(k))

    return jnp.concatenate([conv(x[:, :n_ctx]), conv(x[:, n_ctx:])], axis=1)


def _row_tile(s):
    for tm in (1088, 1024, 544, 512, 256, 128):
        if s % tm == 0:
            return tm
    return s


def kernel(x, c, ctx, c_ctx, norm_g, w_mod, b_mod, w_in, att_qnorm_g, att_knorm_g, dn_conv_w,
           dn_a_log, dn_dt_bias, dn_onorm_g, ret_gnorm_g, w_branch, w_out):
    b, t, d = x.shape
    n_ctx = ctx.shape[1]
    depth = w_in.shape[0]
    s = n_ctx + t
    cols = _Cols(d)
    nh = cols.n_heads
    aw = nh * HEAD_DIM
    kvh = nh // ATT_GROUPS
    tn = 512
    tm = _row_tile(s)
    scale = HEAD_DIM ** -0.5

    xs = jnp.concatenate([ctx, x], axis=1)
    c2, s2 = _rope_tables(n_ctx, t)

    rows = -(-(b + 1) // 8) * 8
    cc = jnp.zeros((rows, d), F32).at[:b].set(c).at[b].set(c_ctx)
    mod = _modulation(cc, w_mod, b_mod)

    decay = jnp.exp(jnp.linspace(math.log(1.0 / 32), math.log(1.0 / 512), nh, dtype=F32))
    log_gamma = jnp.log1p(-decay)
    lg = jnp.stack([log_gamma, log_gamma[::-1]])

    off = cols.off
    for l in range(depth):
        m3 = mod[l].reshape(rows, 3, d)
        mod_lat, mod_ctx = m3[:b], m3[b]
        w = _prep_w_in(w_in[l], cols, tn)
        p = _in_projection(xs, norm_g[l], mod_lat, mod_ctx, w, n_ctx, tm, tn)

        qn = _deinterleave(att_qnorm_g[l], 1)
        kn = _deinterleave(att_knorm_g[l], 1)

        def headnorm(a, g):
            return a * lax.rsqrt(jnp.mean(a * a, axis=-1, keepdims=True) + EPS) * g

        q = p[..., off["att_q"]:off["att_q"] + aw].reshape(b, s, nh, HEAD_DIM)
        q = (_rope(headnorm(q, qn), c2, s2) * scale).astype(BF16).reshape(b, s, aw)
        k = p[..., off["att_k"]:off["att_k"] + kvh * HEAD_DIM].reshape(b, s, kvh, HEAD_DIM)
        k = _rope(headnorm(k, kn), c2, s2).astype(BF16).reshape(b, s, kvh * HEAD_DIM)
        ya = _attention(q, k, p, n_ctx, off["att_v"], off["att_z"])

        qkv = _silu(_segment_conv(p[..., off["dn_qkv"]:off["dn_qkv"] + 3 * aw], dn_conv_w[l], n_ctx))
        l2 = lambda a: a * lax.rsqrt(jnp.sum(a * a, axis=-1, keepdims=True) + EPS)
        dq = (l2(qkv[..., :aw].reshape(b, s, nh, HEAD_DIM)) * scale).reshape(b, s, aw)
        dk = l2(qkv[..., aw:2 * aw].reshape(b, s, nh, HEAD_DIM)).reshape(b, s, aw)
        dv = qkv[..., 2 * aw:]
        ba = p[..., off["dn_b"]:off["dn_b"] + 4 * nh]
        bar = jnp.transpose(ba.reshape(b, s // CHUNK, CHUNK, 4, nh), (0, 4, 1, 3, 2))
        bar = jnp.pad(bar, ((0, 0), (0, 0), (0, 0), (0, 4), (0, 0)))
        nega = -jnp.exp(dn_a_log[l].astype(F32))
        yd = _gdn(nega, dn_dt_bias[l].astype(F32), dq, dk, dv, _chunked_t(dk, nh), bar, p,
                  dn_onorm_g[l], n_ctx, off["dn_b"], off["dn_z"])

        rq = p[..., off["ret_q"]:off["ret_q"] + aw].reshape(b, s, nh, HEAD_DIM)
        rq = _rope(rq, c2, s2).reshape(b, s, aw)
        rk = p[..., off["ret_k"]:off["ret_k"] + aw].reshape(b, s, nh, HEAD_DIM)
        rk = (_rope(rk, c2, s2) * scale).reshape(b, s, aw)
        yr = _retention(lg, rq, _chunked_t(rk, nh), p, ret_gnorm_g[l], n_ctx, off["ret_v"], off["ret_z"])

        merged = _merge(ya, yd, yr, w_branch[l].astype(BF16), p, off["gates"], tm, tn)
        xs = _out_projection(merged, w_out[l].astype(BF16), xs, mod_lat[:, 2:3, :], mod_ctx[2:3, :],
                             n_ctx, tm, tn)

    return xs[:, n_ctx:, :]
```

```python
import functools
import math

import jax
import jax.numpy as jnp
from jax import lax
from jax.experimental import pallas as pl
from jax.experimental.pallas import tpu as pltpu

F32 = jnp.float32
BF16 = jnp.bfloat16

HEAD_DIM = 128
GRID_W = 64
ATT_GROUPS = 4
CHUNK = 64
ROPE_THETA = 10000.0
EPS = 1e-6
N_BRANCH = 3
V7X_VMEM_LIMIT = 56 * 1024 * 1024


def _sigmoid(x):
    return 1.0 / (1.0 + jnp.exp(-x))


def _silu(x):
    return x * _sigmoid(x)


def _softplus(x):
    return jnp.maximum(x, 0.0) + jnp.log1p(jnp.exp(-jnp.abs(x)))


def _dot(a, b):
    return jnp.dot(a.astype(BF16), b.astype(BF16), preferred_element_type=F32)


def _dot_nt(a, b):
    return lax.dot_general(a.astype(BF16), b.astype(BF16), (((1,), (1,)), ((), ())),
                           preferred_element_type=F32)


def _split3(x):
    hi = x.astype(BF16)
    r = x - hi.astype(F32)
    mid = r.astype(BF16)
    lo = (r - mid.astype(F32)).astype(BF16)
    return hi, mid, lo


def _dot_sel_l(sel, x):
    hi, mid, lo = _split3(x)
    d = lambda v: jnp.dot(sel, v, preferred_element_type=F32)
    return d(hi) + (d(mid) + d(lo))


def _cparams(sem):
    return pltpu.CompilerParams(dimension_semantics=sem, vmem_limit_bytes=V7X_VMEM_LIMIT)


def _mod_kernel(c_ref, w_ref, b_ref, o_ref):
    a = _silu(c_ref[...])
    o_ref[0] = _dot(a, w_ref[0]) + b_ref[0]


def _modulation(cc, w_mod, b_mod, tn=512):
    depth, d, n = w_mod.shape
    rows = cc.shape[0]
    return pl.pallas_call(
        _mod_kernel,
        grid=(depth, n // tn),
        in_specs=[pl.BlockSpec((rows, d), lambda l, j: (0, 0)),
                  pl.BlockSpec((1, d, tn), lambda l, j: (l, 0, j)),
                  pl.BlockSpec((1, 1, tn), lambda l, j: (l, 0, j))],
        out_specs=pl.BlockSpec((1, rows, tn), lambda l, j: (l, 0, j)),
        out_shape=jax.ShapeDtypeStruct((depth, rows, n), F32),
        compiler_params=_cparams(("parallel", "parallel")),
        name="modulation",
    )(cc, w_mod, b_mod.reshape(depth, 1, n))


def _inproj_kernel(x_ref, g_ref, lat_ref, ctx_ref, w_ref, o_ref, h_scr, *, n_ctx, tm, d):
    i = pl.program_id(1)
    j = pl.program_id(2)

    @pl.when(j == 0)
    def _():
        x = x_ref[0]
        y = x * lax.rsqrt(jnp.mean(x * x, axis=-1, keepdims=True) + EPS) * g_ref[...]
        row = i * tm + lax.broadcasted_iota(jnp.int32, (tm, 1), 0)
        is_ctx = row < n_ctx
        shift = jnp.where(is_ctx, ctx_ref[0:1, :], lat_ref[0, 0:1, :])
        scale = jnp.where(is_ctx, ctx_ref[1:2, :], lat_ref[0, 1:2, :])
        h_scr[...] = (y * (1.0 + scale) + shift).astype(BF16)

    o_ref[0] = jnp.dot(h_scr[...], w_ref[...], preferred_element_type=F32).astype(o_ref.dtype)


def _in_projection(xs, norm_g, mod_lat, mod_ctx, w, n_ctx, tm, tn=512):
    b, s, d = xs.shape
    n = w.shape[1]
    kern = functools.partial(_inproj_kernel, n_ctx=n_ctx, tm=tm, d=d)
    return pl.pallas_call(
        kern,
        grid=(b, s // tm, n // tn),
        in_specs=[pl.BlockSpec((1, tm, d), lambda bb, i, j: (bb, i, 0)),
                  pl.BlockSpec((1, d), lambda bb, i, j: (0, 0)),
                  pl.BlockSpec((1, 3, d), lambda bb, i, j: (bb, 0, 0)),
                  pl.BlockSpec((3, d), lambda bb, i, j: (0, 0)),
                  pl.BlockSpec((d, tn), lambda bb, i, j: (0, j))],
        out_specs=pl.BlockSpec((1, tm, tn), lambda bb, i, j: (bb, i, j)),
        out_shape=jax.ShapeDtypeStruct((b, s, n), F32),
        scratch_shapes=[pltpu.VMEM((tm, d), BF16)],
        compiler_params=_cparams(("parallel", "parallel", "arbitrary")),
        name="in_projection",
    )(xs, norm_g.reshape(1, d), mod_lat, mod_ctx, w)


def _attn_kernel(q_ref, k_ref, v_ref, z_ref, y_ref, *, n_ctx):
    i = pl.program_id(2)
    q = q_ref[0]

    def attend(kv_len):
        k = k_ref[0, :kv_len, :]
        v = v_ref[0, :kv_len, :].astype(BF16)
        s = _dot_nt(q, k)
        m = jnp.max(s, axis=-1, keepdims=True)
        p = jnp.exp(s - m)
        l = jnp.sum(p, axis=-1, keepdims=True)
        o = jnp.dot(p.astype(BF16), v, preferred_element_type=F32) / l
        y_ref[0] = (o * _silu(z_ref[0])).astype(y_ref.dtype)

    @pl.when(i == 0)
    def _():
        attend(n_ctx)

    @pl.when(i > 0)
    def _():
        attend(k_ref.shape[1])


def _attention(q, k, p, n_ctx, col_v, col_z):
    b, s, aw = q.shape
    nh = aw // HEAD_DIM
    tq = n_ctx
    kern = functools.partial(_attn_kernel, n_ctx=n_ctx)
    vb, zb = col_v // HEAD_DIM, col_z // HEAD_DIM
    return pl.pallas_call(
        kern,
        grid=(b, nh, s // tq),
        in_specs=[pl.BlockSpec((1, tq, HEAD_DIM), lambda bb, h, i: (bb, i, h)),
                  pl.BlockSpec((1, s, HEAD_DIM), lambda bb, h, i: (bb, 0, h // ATT_GROUPS)),
                  pl.BlockSpec((1, s, HEAD_DIM), lambda bb, h, i: (bb, 0, vb + h // ATT_GROUPS)),
                  pl.BlockSpec((1, tq, HEAD_DIM), lambda bb, h, i: (bb, i, zb + h))],
        out_specs=pl.BlockSpec((1, tq, HEAD_DIM), lambda bb, h, i: (bb, i, h)),
        out_shape=jax.ShapeDtypeStruct((b, s, aw), BF16),
        compiler_params=_cparams(("parallel", "parallel", "arbitrary")),
        name="attention",
    )(q, k, p, p)


def _backward_chunk(n, n_chunks, n_ctx_chunks):
    return jnp.where(n < n_ctx_chunks, n_ctx_chunks - 1 - n, n_chunks + n_ctx_chunks - 1 - n)


def _gdn_gates_kernel(raw_ref, nega_ref, dtb_ref, o_ref, *, n_chunks, n_heads):
    L = CHUNK
    ii = lax.broadcasted_iota(jnp.int32, (L, L), 0)
    jj = lax.broadcasted_iota(jnp.int32, (L, L), 1)
    lower_b = jnp.where(ii >= jj, 1.0, 0.0).astype(BF16)
    upper_b = jnp.where(ii <= jj, 1.0, 0.0).astype(BF16)
    lane = lax.broadcasted_iota(jnp.int32, (L, HEAD_DIM), 1)

    def body(c, _):
        r0 = pl.multiple_of(c * L, L)
        raw = raw_ref[0, pl.ds(r0, L), :]
        g = nega_ref[...] * _softplus(raw + dtb_ref[...])
        prefix = _dot_sel_l(lower_b, g)
        suffix = _dot_sel_l(upper_b, g)
        o_ref[0, pl.ds(r0, L), :] = jnp.where(lane < 2 * n_heads, _sigmoid(raw),
                                              jnp.where(lane < 3 * n_heads, prefix, suffix))
        return 0

    lax.fori_loop(0, n_chunks, body, 0)


def _gdn_gates(p, nega_lane, dtb_lane, col_ba, n_heads):
    b, s, _ = p.shape
    kern = functools.partial(_gdn_gates_kernel, n_chunks=s // CHUNK, n_heads=n_heads)
    return pl.pallas_call(
        kern,
        grid=(b,),
        in_specs=[pl.BlockSpec((1, s, HEAD_DIM), lambda bb: (bb, 0, col_ba // HEAD_DIM)),
                  pl.BlockSpec((1, HEAD_DIM), lambda bb: (0, 0)),
                  pl.BlockSpec((1, HEAD_DIM), lambda bb: (0, 0))],
        out_specs=pl.BlockSpec((1, s, HEAD_DIM), lambda bb: (bb, 0, 0)),
        out_shape=jax.ShapeDtypeStruct((b, s, HEAD_DIM), F32),
        compiler_params=_cparams(("parallel",)),
        name="gdn_gates",
    )(p, nega_lane, dtb_lane)


def _unit_tri_inverse_minus_eye(m, blk8, level_masks):
    nm = jnp.where(blk8, -m, 0.0)
    n2 = _dot(nm, nm)
    n4 = _dot(n2, n2)
    e = nm
    e = e + n2 + _dot(e, n2)
    e = e + n4 + _dot(e, n4)
    for cm in level_masks:
        c = jnp.where(cm, m, 0.0)
        tc = c + _dot(e, c)
        e = e - (tc + _dot(tc, e))
    return e


def _gdn_kernel(q_ref, k_ref, v_ref, gate_ref, z_ref, g_ref, y_ref,
                o_scr, wq_scr, u_scr, qk_scr, kdt_scr, gl_scr, *, n_chunks, n_ctx_chunks, n_heads):
    h = pl.program_id(1)
    L = CHUNK
    L2 = 2 * L
    ii = lax.broadcasted_iota(jnp.int32, (L2, L2), 0)
    jj = lax.broadcasted_iota(jnp.int32, (L2, L2), 1)
    lane = lax.broadcasted_iota(jnp.int32, (L, HEAD_DIM), 1)
    up = ii < L
    same64 = (ii >> 6) == (jj >> 6)
    same32 = (ii >> 5) == (jj >> 5)
    same16 = (ii >> 4) == (jj >> 4)
    same8 = (ii >> 3) == (jj >> 3)
    down = jnp.logical_not(up)
    incl = same64 & ((up & (ii >= jj)) | (down & (ii <= jj)))
    strict = same64 & ((up & (ii > jj)) | (down & (ii < jj)))
    levels = (same16 & jnp.logical_not(same8), same32 & jnp.logical_not(same16),
              same64 & jnp.logical_not(same32))
    row8 = lax.broadcasted_iota(jnp.int32, (8, HEAD_DIM), 0)

    def pick(gates, idx):
        return jnp.sum(jnp.where(lane == idx, gates, 0.0), axis=1, keepdims=True)

    def prepare(c, _):
        r0 = pl.multiple_of(c * L, L)
        q = q_ref[0, pl.ds(r0, L), :]
        k = k_ref[0, pl.ds(r0, L), :]
        v = v_ref[0, pl.ds(r0, L), :]
        gates = gate_ref[0, pl.ds(r0, L), :]
        beta2 = jnp.concatenate([pick(gates, h), pick(gates, n_heads + h)], axis=0)
        pf = pick(gates, 2 * n_heads + h)
        pb = pick(gates, 3 * n_heads + h)
        p2 = jnp.concatenate([pf, pb], axis=0)
        gtot2 = jnp.concatenate([jnp.broadcast_to(pf[L - 1:L], (L, 1)),
                                 jnp.broadcast_to(pb[0:1], (L, 1))], axis=0)
        pcol = jnp.broadcast_to(p2, (L2, L2))
        diff = pcol - pcol.T
        decay = jnp.where(incl, jnp.exp(jnp.where(incl, diff, 0.0)), 0.0)
        k2 = jnp.concatenate([k, k], axis=0)
        q2 = jnp.concatenate([q, q], axis=0)
        v2 = jnp.concatenate([v, v], axis=0)
        a = _dot_nt(jnp.concatenate([k2, q2], axis=0), k2)
        m = jnp.where(strict, beta2 * a[:L2] * decay, 0.0)
        qk = a[L2:] * decay
        e = _unit_tri_inverse_minus_eye(m, same8, levels)
        e_in = jnp.exp(p2)
        rhs = jnp.concatenate([v2 * beta2, k2 * (beta2 * e_in)], axis=1)
        uw = rhs + _dot(e, rhs)
        u, w = uw[:, :HEAD_DIM], uw[:, HEAD_DIM:]
        qd = q2 * e_in
        kd = k2 * jnp.exp(gtot2 - p2)
        u_scr[c] = u
        qk_scr[c] = qk.astype(BF16)
        kdt_scr[c] = kd.T.astype(BF16)
        wq_scr[0, c] = jnp.concatenate([w[:L], qd[:L]], axis=0).astype(BF16)
        wq_scr[1, c] = jnp.concatenate([w[L:], qd[L:]], axis=0).astype(BF16)
        gl_scr[c] = jnp.exp(jnp.where(row8 == 0, jnp.broadcast_to(pf[L - 1:L], (8, HEAD_DIM)),
                                      jnp.broadcast_to(pb[0:1], (8, HEAD_DIM))))
        return 0

    lax.fori_loop(0, n_chunks, prepare, 0, unroll=2)

    o_scr[...] = jnp.zeros_like(o_scr)
    zeros = jnp.zeros((L, HEAD_DIM), F32)

    def advance(c, d, state):
        ws = _dot(wq_scr[d, c], state)
        v_new = u_scr[c, d * L:(d + 1) * L, :] - ws[:L]
        vpad = jnp.concatenate([v_new, zeros] if d == 0 else [zeros, v_new], axis=0).astype(BF16)
        o = ws[L:] + jnp.dot(qk_scr[c, d * L:(d + 1) * L, :], vpad, preferred_element_type=F32)
        state = state * gl_scr[c, d:d + 1, :] + jnp.dot(kdt_scr[c], vpad, preferred_element_type=F32)
        r0 = pl.multiple_of(c * L, L)
        o_scr[pl.ds(r0, L), :] += o
        return state

    def body(n, carry):
        sf, sb = carry
        sf = advance(n, 0, sf)
        sb = advance(_backward_chunk(n, n_chunks, n_ctx_chunks), 1, sb)
        return sf, sb

    zero = jnp.zeros((HEAD_DIM, HEAD_DIM), F32)
    lax.fori_loop(0, n_chunks, body, (zero, zero))

    def epilogue(n, _):
        r0 = pl.multiple_of(n * L, L)
        o = o_scr[pl.ds(r0, L), :]
        y = o * lax.rsqrt(jnp.mean(o * o, axis=-1, keepdims=True) + EPS) * g_ref[...]
        y_ref[0, pl.ds(r0, L), :] = (y * _silu(z_ref[0, pl.ds(r0, L), :])).astype(y_ref.dtype)
        return 0

    lax.fori_loop(0, n_chunks, epilogue, 0)


def _gdn(q, k, v, gates, p, onorm_g, n_ctx, col_z):
    b, s, w = q.shape
    nh = w // HEAD_DIM
    nc = s // CHUNK
    kern = functools.partial(_gdn_kernel, n_chunks=nc, n_ctx_chunks=n_ctx // CHUNK, n_heads=nh)
    seq = lambda off: pl.BlockSpec((1, s, HEAD_DIM), lambda bb, h: (bb, 0, off + h))
    return pl.pallas_call(
        kern,
        grid=(b, nh),
        in_specs=[seq(0), seq(0), seq(0),
                  pl.BlockSpec((1, s, HEAD_DIM), lambda bb, h: (bb, 0, 0)),
                  seq(col_z // HEAD_DIM),
                  pl.BlockSpec((1, HEAD_DIM), lambda bb, h: (0, 0))],
        out_specs=seq(0),
        out_shape=jax.ShapeDtypeStruct((b, s, w), BF16),
        scratch_shapes=[pltpu.VMEM((s, HEAD_DIM), F32),
                        pltpu.VMEM((2, nc, 2 * CHUNK, HEAD_DIM), BF16),
                        pltpu.VMEM((nc, 2 * CHUNK, HEAD_DIM), F32),
                        pltpu.VMEM((nc, 2 * CHUNK, 2 * CHUNK), BF16),
                        pltpu.VMEM((nc, HEAD_DIM, 2 * CHUNK), BF16),
                        pltpu.VMEM((nc, 8, HEAD_DIM), F32)],
        compiler_params=_cparams(("parallel", "arbitrary")),
        name="gated_deltanet",
    )(q, k, v, gates, p, onorm_g.reshape(1, HEAD_DIM))


def _ret_kernel(lg_ref, q_ref, k_ref, v_ref, z_ref, g_ref, y_ref, o_scr, kzt_scr,
                *, n_chunks, n_ctx_chunks):
    h = pl.program_id(1)
    L = CHUNK
    L2 = 2 * L
    ii = lax.broadcasted_iota(jnp.int32, (L, L), 0)
    jj = lax.broadcasted_iota(jnp.int32, (L, L), 1)
    row_i = lax.broadcasted_iota(jnp.int32, (L, HEAD_DIM), 0).astype(F32)
    row2 = lax.broadcasted_iota(jnp.int32, (L2, HEAD_DIM), 0).astype(F32)
    dist = (ii - jj).astype(F32)
    lgs = (lg_ref[0, h], lg_ref[1, h])
    dsum = (jnp.where(ii >= jj, jnp.exp(jnp.maximum(dist, 0.0) * lgs[0]), 0.0)
            + jnp.where(ii <= jj, jnp.exp(jnp.maximum(-dist, 0.0) * lgs[1]), 0.0))
    xi = (jnp.exp((row_i + 1.0) * lgs[0]), jnp.exp((L - row_i) * lgs[1]))
    zeta2 = jnp.where(row2 < L, jnp.exp((L - 1.0 - row2) * lgs[0]), jnp.exp((row2 - L) * lgs[1]))
    gch = tuple(jnp.exp(jnp.full((1, HEAD_DIM), float(L), F32) * lg) for lg in lgs)
    zeros = jnp.zeros((L, HEAD_DIM), F32)

    def prepare(c, _):
        r0 = pl.multiple_of(c * L, L)
        q = q_ref[0, pl.ds(r0, L), :]
        k = k_ref[0, pl.ds(r0, L), :]
        v = v_ref[0, pl.ds(r0, L), :]
        o_scr[pl.ds(r0, L), :] = _dot(_dot_nt(q, k) * dsum, v)
        kz = jnp.concatenate([k, k], axis=0) * zeta2
        kzt_scr[c] = kz.T.astype(BF16)
        return 0

    lax.fori_loop(0, n_chunks, prepare, 0, unroll=2)

    def advance(c, d, state):
        r0 = pl.multiple_of(c * L, L)
        q = q_ref[0, pl.ds(r0, L), :]
        v = v_ref[0, pl.ds(r0, L), :]
        vpad = jnp.concatenate([v, zeros] if d == 0 else [zeros, v], axis=0).astype(BF16)
        o_scr[pl.ds(r0, L), :] += xi[d] * _dot(q, state)
        return state * gch[d] + jnp.dot(kzt_scr[c], vpad, preferred_element_type=F32)

    def body(n, carry):
        sf, sb = carry
        sf = advance(n, 0, sf)
        sb = advance(_backward_chunk(n, n_chunks, n_ctx_chunks), 1, sb)
        return sf, sb

    zero = jnp.zeros((HEAD_DIM, HEAD_DIM), F32)
    lax.fori_loop(0, n_chunks, body, (zero, zero), unroll=2)

    def epilogue(n, _):
        r0 = pl.multiple_of(n * L, L)
        o = o_scr[pl.ds(r0, L), :]
        mu = jnp.mean(o, axis=-1, keepdims=True)
        var = jnp.mean(jnp.square(o - mu), axis=-1, keepdims=True)
        y = (o - mu) * lax.rsqrt(var + EPS) * g_ref[...]
        y_ref[0, pl.ds(r0, L), :] = (y * _silu(z_ref[0, pl.ds(r0, L), :])).astype(y_ref.dtype)
        return 0

    lax.fori_loop(0, n_chunks, epilogue, 0)


def _retention(lg, q, k, p, gnorm_g, n_ctx, col_v, col_z):
    b, s, w = q.shape
    nh = w // HEAD_DIM
    nc = s // CHUNK
    kern = functools.partial(_ret_kernel, n_chunks=nc, n_ctx_chunks=n_ctx // CHUNK)
    seq = lambda off: pl.BlockSpec((1, s, HEAD_DIM), lambda bb, h: (bb, 0, off + h))
    return pl.pallas_call(
        kern,
        grid=(b, nh),
        in_specs=[pl.BlockSpec(memory_space=pltpu.SMEM), seq(0), seq(0), seq(col_v // HEAD_DIM),
                  seq(col_z // HEAD_DIM),
                  pl.BlockSpec((1, HEAD_DIM), lambda bb, h: (0, h))],
        out_specs=seq(0),
        out_shape=jax.ShapeDtypeStruct((b, s, w), BF16),
        scratch_shapes=[pltpu.VMEM((s, HEAD_DIM), F32),
                        pltpu.VMEM((nc, HEAD_DIM, 2 * CHUNK), BF16)],
        compiler_params=_cparams(("parallel", "arbitrary")),
        name="retention",
    )(lg, q, k, p, p, gnorm_g.reshape(1, w))


def _merge_kernel(ya_ref, yd_ref, yr_ref, wa_ref, wd_ref, wr_ref, ga_ref, gd_ref, gr_ref, o_ref):
    acc = _sigmoid(ga_ref[0]) * jnp.dot(ya_ref[0], wa_ref[0], preferred_element_type=F32)
    acc = acc + _sigmoid(gd_ref[0]) * jnp.dot(yd_ref[0], wd_ref[0], preferred_element_type=F32)
    acc = acc + _sigmoid(gr_ref[0]) * jnp.dot(yr_ref[0], wr_ref[0], preferred_element_type=F32)
    o_ref[0] = acc.astype(o_ref.dtype)


def _merge(ya, yd, yr, w_branch, p, col_gates, tm, tn=512):
    b, s, bw = ya.shape
    d = w_branch.shape[2]
    gb = col_gates // tn
    ysp = pl.BlockSpec((1, tm, bw), lambda bb, i, j: (bb, i, 0))
    wsp = lambda br: pl.BlockSpec((1, bw, tn), lambda bb, i, j: (br, 0, j))
    gsp = lambda br: pl.BlockSpec((1, tm, tn), lambda bb, i, j: (bb, i, gb + br * (d // tn) + j))
    return pl.pallas_call(
        _merge_kernel,
        grid=(b, s // tm, d // tn),
        in_specs=[ysp, ysp, ysp, wsp(0), wsp(1), wsp(2), gsp(0), gsp(1), gsp(2)],
        out_specs=pl.BlockSpec((1, tm, tn), lambda bb, i, j: (bb, i, j)),
        out_shape=jax.ShapeDtypeStruct((b, s, d), BF16),
        compiler_params=_cparams(("parallel", "parallel", "arbitrary")),
        name="branch_merge",
    )(ya, yd, yr, w_branch, w_branch, w_branch, p, p, p)


def _outproj_kernel(m_ref, w_ref, x_ref, lat_ref, ctx_ref, o_ref, *, n_ctx, tm):
    i = pl.program_id(1)
    row = i * tm + lax.broadcasted_iota(jnp.int32, (tm, 1), 0)
    gate = jnp.where(row < n_ctx, ctx_ref[...], lat_ref[0])
    o_ref[0] = x_ref[0] + gate * jnp.dot(m_ref[0], w_ref[...], preferred_element_type=F32)


def _out_projection(merged, w_out, xs, gate_lat, gate_ctx, n_ctx, tm, tn=512):
    b, s, d = xs.shape
    kern = functools.partial(_outproj_kernel, n_ctx=n_ctx, tm=tm)
    return pl.pallas_call(
        kern,
        grid=(b, s // tm, d // tn),
        in_specs=[pl.BlockSpec((1, tm, d), lambda bb, i, j: (bb, i, 0)),
                  pl.BlockSpec((d, tn), lambda bb, i, j: (0, j)),
                  pl.BlockSpec((1, tm, tn), lambda bb, i, j: (bb, i, j)),
                  pl.BlockSpec((1, 1, tn), lambda bb, i, j: (bb, 0, j)),
                  pl.BlockSpec((1, tn), lambda bb, i, j: (0, j))],
        out_specs=pl.BlockSpec((1, tm, tn), lambda bb, i, j: (bb, i, j)),
        out_shape=jax.ShapeDtypeStruct((b, s, d), F32),
        compiler_params=_cparams(("parallel", "parallel", "arbitrary")),
        name="out_projection",
    )(merged, w_out, xs, gate_lat, gate_ctx)


def _deinterleave(w, n_heads):
    lead = w.shape[:-1]
    w = w.reshape(*lead, n_heads, HEAD_DIM // 2, 2)
    return jnp.swapaxes(w, -1, -2).reshape(*lead, n_heads * HEAD_DIM)


class _Cols:
    def __init__(self, d_model):
        aw = d_model // 2
        kvw = aw // ATT_GROUPS
        nh = aw // HEAD_DIM
        self.src = dict(att_q=aw, att_k=kvw, att_v=kvw, att_z=aw, dn_qkv=3 * aw, dn_z=aw,
                        dn_b=2 * nh, dn_a=2 * nh, ret_q=aw, ret_k=aw, ret_v=aw, ret_z=aw,
                        gates=N_BRANCH * d_model)
        order = ("att_q", "att_k", "att_v", "att_z", "dn_qkv", "dn_z", "ret_q", "ret_k", "ret_v",
                 "ret_z", "gates", "dn_b", "dn_a")
        self.order = order
        self.off = {}
        o = 0
        for name in order:
            self.off[name] = o
            o += self.src[name]
        self.used = o
        self.n_heads = nh


def _prep_w_in(w_in, cols, tn):
    offs, o = {}, 0
    for name, width in cols.src.items():
        offs[name] = o
        o += width
    parts = []
    for name in cols.order:
        seg = w_in[:, offs[name]:offs[name] + cols.src[name]]
        if name in ("att_q", "att_k", "ret_q", "ret_k"):
            seg = _deinterleave(seg, cols.src[name] // HEAD_DIM)
        parts.append(seg)
    pad = (-cols.used) % tn
    if pad:
        parts.append(jnp.zeros((w_in.shape[0], pad), w_in.dtype))
    return jnp.concatenate(parts, axis=1).astype(BF16)


def _rope_tables(n_ctx, t):
    rows = t // GRID_W
    row, col = jnp.meshgrid(jnp.arange(rows, dtype=F32), jnp.arange(GRID_W, dtype=F32), indexing="ij")
    n_freq = HEAD_DIM // 4
    inv = ROPE_THETA ** (-jnp.arange(n_freq, dtype=F32) / n_freq)
    ang = jnp.concatenate([row.reshape(-1, 1) * inv, col.reshape(-1, 1) * inv], axis=-1)
    cos, sin = jnp.cos(ang), jnp.sin(ang)
    cos = jnp.concatenate([jnp.ones((n_ctx, HEAD_DIM // 2), F32), cos], axis=0)
    sin = jnp.concatenate([jnp.zeros((n_ctx, HEAD_DIM // 2), F32), sin], axis=0)
    return jnp.concatenate([cos, cos], axis=-1), jnp.concatenate([-sin, sin], axis=-1)


def _rope(x, c2, s2):
    return x * c2[None, :, None, :] + jnp.roll(x, HEAD_DIM // 2, axis=-1) * s2[None, :, None, :]


def _segment_conv(x, w, n_ctx):
    k = w.shape[0]

    def conv(seg):
        n = seg.shape[1]
        pad = jnp.pad(seg, ((0, 0), (k // 2, k // 2), (0, 0)))
        return sum(pad[:, j:j + n, :] * w[j] for j in range(k))

    return jnp.concatenate([conv(x[:, :n_ctx]), conv(x[:, n_ctx:])], axis=1)


def _row_tile(s):
    for tm in (1088, 1024, 544, 512, 256, 128):
        if s % tm == 0:
            return tm
    return s


def kernel(x, c, ctx, c_ctx, norm_g, w_mod, b_mod, w_in, att_qnorm_g, att_knorm_g, dn_conv_w,
           dn_a_log, dn_dt_bias, dn_onorm_g, ret_gnorm_g, w_branch, w_out):
    b, t, d = x.shape
    n_ctx = ctx.shape[1]
    depth = w_in.shape[0]
    s = n_ctx + t
    cols = _Cols(d)
    nh = cols.n_heads
    aw = nh * HEAD_DIM
    kvh = nh // ATT_GROUPS
    tn = 512
    tm = _row_tile(s)
    scale = HEAD_DIM ** -0.5

    xs = jnp.concatenate([ctx, x], axis=1)
    c2, s2 = _rope_tables(n_ctx, t)

    rows = -(-(b + 1) // 8) * 8
    cc = jnp.zeros((rows, d), F32).at[:b].set(c).at[b].set(c_ctx)
    mod = _modulation(cc, w_mod, b_mod)

    decay = jnp.exp(jnp.linspace(math.log(1.0 / 32), math.log(1.0 / 512), nh, dtype=F32))
    log_gamma = jnp.log1p(-decay)
    lg = jnp.stack([log_gamma, log_gamma[::-1]])

    off = cols.off
    for l in range(depth):
        m3 = mod[l].reshape(rows, 3, d)
        mod_lat, mod_ctx = m3[:b], m3[b]
        w = _prep_w_in(w_in[l], cols, tn)
        p = _in_projection(xs, norm_g[l], mod_lat, mod_ctx, w, n_ctx, tm, tn)

        qn = _deinterleave(att_qnorm_g[l], 1)
        kn = _deinterleave(att_knorm_g[l], 1)

        def headnorm(a, g):
            return a * lax.rsqrt(jnp.mean(a * a, axis=-1, keepdims=True) + EPS) * g

        q = p[..., off["att_q"]:off["att_q"] + aw].reshape(b, s, nh, HEAD_DIM)
        q = (_rope(headnorm(q, qn), c2, s2) * scale).astype(BF16).reshape(b, s, aw)
        k = p[..., off["att_k"]:off["att_k"] + kvh * HEAD_DIM].reshape(b, s, kvh, HEAD_DIM)
        k = _rope(headnorm(k, kn), c2, s2).astype(BF16).reshape(b, s, kvh * HEAD_DIM)
        ya = _attention(q, k, p, n_ctx, off["att_v"], off["att_z"])

        qkv = _silu(_segment_conv(p[..., off["dn_qkv"]:off["dn_qkv"] + 3 * aw], dn_conv_w[l], n_ctx))
        l2 = lambda a: a * lax.rsqrt(jnp.sum(a * a, axis=-1, keepdims=True) + EPS)
        dq = (l2(qkv[..., :aw].reshape(b, s, nh, HEAD_DIM)) * scale).reshape(b, s, aw)
        dk = l2(qkv[..., aw:2 * aw].reshape(b, s, nh, HEAD_DIM)).reshape(b, s, aw)
        dv = qkv[..., 2 * aw:]
        pad_l = jnp.zeros((2 * nh,), F32)
        pad_r = jnp.zeros((HEAD_DIM - 4 * nh,), F32)
        nega_lane = jnp.concatenate([pad_l, -jnp.exp(dn_a_log[l].astype(F32)).reshape(-1), pad_r])
        dtb_lane = jnp.concatenate([pad_l, dn_dt_bias[l].astype(F32).reshape(-1), pad_r])
        gates = _gdn_gates(p, nega_lane.reshape(1, HEAD_DIM), dtb_lane.reshape(1, HEAD_DIM), off["dn_b"], nh)
        yd = _gdn(dq, dk, dv, gates, p, dn_onorm_g[l], n_ctx, off["dn_z"])

        rq = p[..., off["ret_q"]:off["ret_q"] + aw].reshape(b, s, nh, HEAD_DIM)
        rq = _rope(rq, c2, s2).reshape(b, s, aw)
        rk = p[..., off["ret_k"]:off["ret_k"] + aw].reshape(b, s, nh, HEAD_DIM)
        rk = (_rope(rk, c2, s2) * scale).reshape(b, s, aw)
        yr = _retention(lg, rq, rk, p, ret_gnorm_g[l], n_ctx, off["ret_v"], off["ret_z"])

        merged = _merge(ya, yd, yr, w_branch[l].astype(BF16), p, off["gates"], tm, tn)
        xs = _out_projection(merged, w_out[l].astype(BF16), xs, mod_lat[:, 2:3, :], mod_ctx[2:3, :],
                             n_ctx, tm, tn)

    return xs[:, n_ctx:, :]
```

```python
import functools
import math

import jax
import jax.numpy as jnp
from jax import lax
from jax.experimental import pallas as pl
from jax.experimental.pallas import tpu as pltpu

F32 = jnp.float32
BF16 = jnp.bfloat16

HEAD_DIM = 128
HALF = HEAD_DIM // 2
GRID_W = 64
ATT_GROUPS = 4
CHUNK = 64
SUBLANES = 8
ROPE_THETA = 10000.0
EPS = 1e-6
N_BRANCH = 3
V7X_VMEM_LIMIT = 56 * 1024 * 1024


def _sigmoid(x):
    return 1.0 / (1.0 + jnp.exp(-x))


def _silu(x):
    return x * _sigmoid(x)


def _softplus(x):
    return jnp.maximum(x, 0.0) + jnp.log1p(jnp.exp(-jnp.abs(x)))


def _dot(a, b):
    return jnp.dot(a.astype(BF16), b.astype(BF16), preferred_element_type=F32)


def _dot_nt(a, b):
    return lax.dot_general(a.astype(BF16), b.astype(BF16), (((1,), (1,)), ((), ())),
                           preferred_element_type=F32)


def _split3(x):
    hi = x.astype(BF16)
    r = x - hi.astype(F32)
    mid = r.astype(BF16)
    lo = (r - mid.astype(F32)).astype(BF16)
    return hi, mid, lo


def _dot_sel_l(sel, x):
    hi, mid, lo = _split3(x)
    d = lambda v: jnp.dot(sel, v, preferred_element_type=F32)
    return d(hi) + (d(mid) + d(lo))


def _rope(x, c2, s2):
    return x * c2 + pltpu.roll(x, HALF, 1) * s2


def _rms(x):
    return x * lax.rsqrt(jnp.mean(x * x, axis=-1, keepdims=True) + EPS)


def _cparams(sem):
    return pltpu.CompilerParams(dimension_semantics=sem, vmem_limit_bytes=V7X_VMEM_LIMIT)


def _group(n_chunks):
    for g in (4, 2):
        if n_chunks % g == 0:
            return g
    return 1


def _mod_kernel(c_ref, w_ref, b_ref, o_ref):
    a = _silu(c_ref[...])
    o_ref[0] = _dot(a, w_ref[0]) + b_ref[0]


def _modulation(cc, w_mod, b_mod, tn=512):
    depth, d, n = w_mod.shape
    rows = cc.shape[0]
    return pl.pallas_call(
        _mod_kernel,
        grid=(depth, n // tn),
        in_specs=[pl.BlockSpec((rows, d), lambda l, j: (0, 0)),
                  pl.BlockSpec((1, d, tn), lambda l, j: (l, 0, j)),
                  pl.BlockSpec((1, 1, tn), lambda l, j: (l, 0, j))],
        out_specs=pl.BlockSpec((1, rows, tn), lambda l, j: (l, 0, j)),
        out_shape=jax.ShapeDtypeStruct((depth, rows, n), F32),
        compiler_params=_cparams(("parallel", "parallel")),
        name="modulation",
    )(cc, w_mod, b_mod.reshape(depth, 1, n))


def _inproj_kernel(x_ref, g_ref, lat_ref, ctx_ref, w_ref, o_ref, h_scr, *, n_ctx, tm, d):
    i = pl.program_id(1)
    j = pl.program_id(2)

    @pl.when(j == 0)
    def _():
        y = _rms(x_ref[0]) * g_ref[...]
        row = i * tm + lax.broadcasted_iota(jnp.int32, (tm, 1), 0)
        is_ctx = row < n_ctx
        shift = jnp.where(is_ctx, ctx_ref[0:1, :], lat_ref[0, 0:1, :])
        scale = jnp.where(is_ctx, ctx_ref[1:2, :], lat_ref[0, 1:2, :])
        h_scr[...] = (y * (1.0 + scale) + shift).astype(BF16)

    o_ref[0] = jnp.dot(h_scr[...], w_ref[...], preferred_element_type=F32).astype(o_ref.dtype)


def _in_projection(xs, norm_g, mod_lat, mod_ctx, w, n_ctx, tm, tn=512):
    b, s, d = xs.shape
    n = w.shape[1]
    kern = functools.partial(_inproj_kernel, n_ctx=n_ctx, tm=tm, d=d)
    return pl.pallas_call(
        kern,
        grid=(b, s // tm, n // tn),
        in_specs=[pl.BlockSpec((1, tm, d), lambda bb, i, j: (bb, i, 0)),
                  pl.BlockSpec((1, d), lambda bb, i, j: (0, 0)),
                  pl.BlockSpec((1, 3, d), lambda bb, i, j: (bb, 0, 0)),
                  pl.BlockSpec((3, d), lambda bb, i, j: (0, 0)),
                  pl.BlockSpec((d, tn), lambda bb, i, j: (0, j))],
        out_specs=pl.BlockSpec((1, tm, tn), lambda bb, i, j: (bb, i, j)),
        out_shape=jax.ShapeDtypeStruct((b, s, n), F32),
        scratch_shapes=[pltpu.VMEM((tm, d), BF16)],
        compiler_params=_cparams(("parallel", "parallel", "arbitrary")),
        name="in_projection",
    )(xs, norm_g.reshape(1, d), mod_lat, mod_ctx, w)


def _attn_kernel(q_ref, k_ref, v_ref, z_ref, c2q_ref, s2q_ref, c2_ref, s2_ref, qn_ref, kn_ref,
                 y_ref, k_scr, v_scr, *, n_ctx, blk):
    h = pl.program_id(1)
    i = pl.program_id(2)
    s = k_ref.shape[1]

    @pl.when((i == 0) & (h % ATT_GROUPS == 0))
    def _():
        def prep(n, _):
            r0 = pl.multiple_of(n * blk, blk)
            k = _rms(k_ref[0, pl.ds(r0, blk), :]) * kn_ref[...]
            k = _rope(k, c2_ref[pl.ds(r0, blk), :], s2_ref[pl.ds(r0, blk), :])
            k_scr[pl.ds(r0, blk), :] = k.astype(BF16)
            v_scr[pl.ds(r0, blk), :] = v_ref[0, pl.ds(r0, blk), :].astype(BF16)
            return 0

        lax.fori_loop(0, s // blk, prep, 0)

    q = _rope(_rms(q_ref[0]) * qn_ref[...], c2q_ref[...], s2q_ref[...])
    q = (q * (HEAD_DIM ** -0.5)).astype(BF16)

    def attend(kv_len):
        sc = lax.dot_general(q, k_scr[:kv_len, :], (((1,), (1,)), ((), ())), preferred_element_type=F32)
        m = jnp.max(sc, axis=-1, keepdims=True)
        p = jnp.exp(sc - m)
        l = jnp.sum(p, axis=-1, keepdims=True)
        o = jnp.dot(p.astype(BF16), v_scr[:kv_len, :], preferred_element_type=F32) / l
        y_ref[0] = (o * _silu(z_ref[0])).astype(y_ref.dtype)

    @pl.when(i == 0)
    def _():
        attend(n_ctx)

    @pl.when(i > 0)
    def _():
        attend(s)


def _attention(p, c2, s2, qn, kn, n_ctx, n_heads, col_q, col_k, col_v, col_z):
    b, s, _ = p.shape
    tq = n_ctx
    kern = functools.partial(_attn_kernel, n_ctx=n_ctx, blk=tq)
    qb, kb, vb, zb = (c // HEAD_DIM for c in (col_q, col_k, col_v, col_z))
    row_blk = lambda off: pl.BlockSpec((1, tq, HEAD_DIM), lambda bb, h, i: (bb, i, off + h))
    kv_blk = lambda off: pl.BlockSpec((1, s, HEAD_DIM), lambda bb, h, i: (bb, 0, off + h // ATT_GROUPS))
    tab_q = pl.BlockSpec((tq, HEAD_DIM), lambda bb, h, i: (i, 0))
    tab = pl.BlockSpec((s, HEAD_DIM), lambda bb, h, i: (0, 0))
    vec = pl.BlockSpec((1, HEAD_DIM), lambda bb, h, i: (0, 0))
    return pl.pallas_call(
        kern,
        grid=(b, n_heads, s // tq),
        in_specs=[row_blk(qb), kv_blk(kb), kv_blk(vb), row_blk(zb), tab_q, tab_q, tab, tab, vec, vec],
        out_specs=row_blk(0),
        out_shape=jax.ShapeDtypeStruct((b, s, n_heads * HEAD_DIM), BF16),
        scratch_shapes=[pltpu.VMEM((s, HEAD_DIM), BF16), pltpu.VMEM((s, HEAD_DIM), BF16)],
        compiler_params=_cparams(("parallel", "arbitrary", "arbitrary")),
        name="attention",
    )(p, p, p, p, c2, s2, c2, s2, qn.reshape(1, HEAD_DIM), kn.reshape(1, HEAD_DIM))


def _backward_chunk(n, n_chunks, n_ctx_chunks):
    return jnp.where(n < n_ctx_chunks, n_ctx_chunks - 1 - n, n_chunks + n_ctx_chunks - 1 - n)


def _gdn_gates_kernel(raw_ref, nega_ref, dtb_ref, o_ref, *, n_chunks, n_heads):
    L = CHUNK
    ii = lax.broadcasted_iota(jnp.int32, (L, L), 0)
    jj = lax.broadcasted_iota(jnp.int32, (L, L), 1)
    lower_b = jnp.where(ii >= jj, 1.0, 0.0).astype(BF16)
    upper_b = jnp.where(ii <= jj, 1.0, 0.0).astype(BF16)
    lane = lax.broadcasted_iota(jnp.int32, (L, HEAD_DIM), 1)

    def body(c, _):
        r0 = pl.multiple_of(c * L, L)
        raw = raw_ref[0, pl.ds(r0, L), :]
        g = nega_ref[...] * _softplus(raw + dtb_ref[...])
        prefix = _dot_sel_l(lower_b, g)
        suffix = _dot_sel_l(upper_b, g)
        o_ref[0, pl.ds(r0, L), :] = jnp.where(lane < 2 * n_heads, _sigmoid(raw),
                                              jnp.where(lane < 3 * n_heads, prefix, suffix))
        return 0

    lax.fori_loop(0, n_chunks, body, 0)


def _gdn_gates(p, nega_lane, dtb_lane, col_ba, n_heads):
    b, s, _ = p.shape
    kern = functools.partial(_gdn_gates_kernel, n_chunks=s // CHUNK, n_heads=n_heads)
    return pl.pallas_call(
        kern,
        grid=(b,),
        in_specs=[pl.BlockSpec((1, s, HEAD_DIM), lambda bb: (bb, 0, col_ba // HEAD_DIM)),
                  pl.BlockSpec((1, HEAD_DIM), lambda bb: (0, 0)),
                  pl.BlockSpec((1, HEAD_DIM), lambda bb: (0, 0))],
        out_specs=pl.BlockSpec((1, s, HEAD_DIM), lambda bb: (bb, 0, 0)),
        out_shape=jax.ShapeDtypeStruct((b, s, HEAD_DIM), F32),
        compiler_params=_cparams(("parallel",)),
        name="gdn_gates",
    )(p, nega_lane, dtb_lane)


def _each(f, *lists):
    return [f(*a) for a in zip(*lists)]


def _unit_tri_inverse_minus_eye(ms, blk8, level_masks):
    nm = _each(lambda m: jnp.where(blk8, -m, 0.0), ms)
    n2 = _each(lambda a: _dot(a, a), nm)
    n4 = _each(lambda a: _dot(a, a), n2)
    e = _each(lambda a, a2: a + a2 + _dot(a, a2), nm, n2)
    e = _each(lambda x, a4: x + a4 + _dot(x, a4), e, n4)
    for cm in level_masks:
        c = _each(lambda m: jnp.where(cm, m, 0.0), ms)
        tc = _each(lambda x, y: y + _dot(x, y), e, c)
        e = _each(lambda x, y: x - (y + _dot(y, x)), e, tc)
    return e


def _gdn_kernel(q_ref, k_ref, v_ref, wq_ref, wk_ref, wv_ref, gate_ref, z_ref, g_ref, y_ref,
                o_scr, aq_scr, b_scr, gl_scr, *, n_chunks, n_ctx_chunks, n_heads, group):
    h = pl.program_id(1)
    L = CHUNK
    L2 = 2 * L
    s_rows = q_ref.shape[1]
    ii = lax.broadcasted_iota(jnp.int32, (L2, L2), 0)
    jj = lax.broadcasted_iota(jnp.int32, (L2, L2), 1)
    lane = lax.broadcasted_iota(jnp.int32, (L, HEAD_DIM), 1)
    up = ii < L
    down = jnp.logical_not(up)
    same64 = (ii >> 6) == (jj >> 6)
    same32 = (ii >> 5) == (jj >> 5)
    same16 = (ii >> 4) == (jj >> 4)
    same8 = (ii >> 3) == (jj >> 3)
    incl = same64 & ((up & (ii >= jj)) | (down & (ii <= jj)))
    strict = same64 & ((up & (ii > jj)) | (down & (ii < jj)))
    levels = (same16 & jnp.logical_not(same8), same32 & jnp.logical_not(same16),
              same64 & jnp.logical_not(same32))
    row8 = lax.broadcasted_iota(jnp.int32, (SUBLANES, HEAD_DIM), 0)
    zeros_wu = jnp.zeros((L, 2 * HEAD_DIM), BF16)
    taps = wq_ref.shape[0]
    halo = taps // 2

    def pick(gates, idx):
        return jnp.sum(jnp.where(lane == idx, gates, 0.0), axis=1, keepdims=True)

    def conv_silu(ref, w_ref, c):
        r0 = pl.multiple_of(c * L, L)
        x = ref[0, pl.ds(r0, L), :]
        lo = pl.multiple_of(jnp.maximum(r0 - SUBLANES, 0), SUBLANES)
        hi = pl.multiple_of(jnp.minimum(r0 + L, s_rows - SUBLANES), SUBLANES)
        seg_first = (c == 0) | (c == n_ctx_chunks)
        seg_last = (c == n_ctx_chunks - 1) | (c == n_chunks - 1)
        prev = ref[0, pl.ds(lo, SUBLANES), :] * jnp.where(seg_first, 0.0, 1.0)
        nxt = ref[0, pl.ds(hi, SUBLANES), :] * jnp.where(seg_last, 0.0, 1.0)
        ext = jnp.concatenate([prev, x, nxt], axis=0)
        w = w_ref[...]
        acc = ext[SUBLANES - halo:SUBLANES - halo + L] * w[0:1]
        for j in range(1, taps):
            acc = acc + ext[SUBLANES - halo + j:SUBLANES - halo + j + L] * w[j:j + 1]
        return _silu(acc)

    def l2(x):
        return x * lax.rsqrt(jnp.sum(x * x, axis=-1, keepdims=True) + EPS)

    def stack2(x):
        return jnp.concatenate([x, x], axis=0)

    def prepare_group(i, _):
        cs = [i * group + g for g in range(group)]
        q = _each(lambda c: stack2(l2(conv_silu(q_ref, wq_ref, c)) * (HEAD_DIM ** -0.5)), cs)
        k2 = _each(lambda c: stack2(l2(conv_silu(k_ref, wk_ref, c))), cs)
        v2 = _each(lambda c: stack2(conv_silu(v_ref, wv_ref, c)), cs)
        gates = _each(lambda c: gate_ref[0, pl.ds(pl.multiple_of(c * L, L), L), :], cs)
        beta2 = _each(lambda g: jnp.concatenate([pick(g, h), pick(g, n_heads + h)], axis=0), gates)
        pf = _each(lambda g: pick(g, 2 * n_heads + h), gates)
        pb = _each(lambda g: pick(g, 3 * n_heads + h), gates)
        p2 = _each(lambda x, y: jnp.concatenate([x, y], axis=0), pf, pb)
        gtot2 = _each(lambda x, y: jnp.concatenate([jnp.broadcast_to(x[L - 1:L], (L, 1)),
                                                    jnp.broadcast_to(y[0:1], (L, 1))], axis=0), pf, pb)
        pcol = _each(lambda x: jnp.broadcast_to(x, (L2, L2)), p2)
        diff = _each(lambda x: x - x.T, pcol)
        decay = _each(lambda x: jnp.where(incl, jnp.exp(jnp.where(incl, x, 0.0)), 0.0), diff)
        a = _each(lambda kk, qq: _dot_nt(jnp.concatenate([kk, qq], axis=0), kk), k2, q)
        m = _each(lambda bt, x, dc: jnp.where(strict, bt * x[:L2] * dc, 0.0), beta2, a, decay)
        qk = _each(lambda x, dc: (x[L2:] * dc).astype(BF16), a, decay)
        e = _unit_tri_inverse_minus_eye(m, same8, levels)
        e_in = _each(jnp.exp, p2)
        rhs = _each(lambda kk, vv, bt, ei: jnp.concatenate([kk * (bt * ei), vv * bt], axis=1),
                    k2, v2, beta2, e_in)
        wu = _each(lambda x, r: (r + _dot(x, r)).astype(BF16), e, rhs)
        kdt = _each(lambda kk, gt, pp: (kk * jnp.exp(gt - pp)).T.astype(BF16), k2, gtot2, p2)
        ab_up = _each(lambda kt, x: jnp.dot(kt, jnp.concatenate([x[:L], zeros_wu], axis=0),
                                            preferred_element_type=F32), kdt, wu)
        ab_dn = _each(lambda kt, x: jnp.dot(kt, jnp.concatenate([zeros_wu, x[L:]], axis=0),
                                            preferred_element_type=F32), kdt, wu)
        qo = _each(lambda x, y: jnp.dot(x, y, preferred_element_type=F32), qk, wu)
        qp = _each(lambda qq, ei, x: qq * ei - x[:, :HEAD_DIM], q, e_in, qo)
        for g, c in enumerate(cs):
            o_scr[pl.ds(pl.multiple_of(c * L, L), L), :] = qo[g][:L, HEAD_DIM:] + qo[g][L:, HEAD_DIM:]
            aq_scr[0, c] = jnp.concatenate([ab_up[g][:, :HEAD_DIM], qp[g][:L]], axis=0).astype(BF16)
            aq_scr[1, c] = jnp.concatenate([ab_dn[g][:, :HEAD_DIM], qp[g][L:]], axis=0).astype(BF16)
            b_scr[0, c] = ab_up[g][:, HEAD_DIM:]
            b_scr[1, c] = ab_dn[g][:, HEAD_DIM:]
            gl_scr[c] = jnp.exp(jnp.where(row8 == 0,
                                          jnp.broadcast_to(pf[g][L - 1:L], (SUBLANES, HEAD_DIM)),
                                          jnp.broadcast_to(pb[g][0:1], (SUBLANES, HEAD_DIM))))
        return 0

    lax.fori_loop(0, n_chunks // group, prepare_group, 0)

    def advance(c, d, state):
        r = jnp.dot(aq_scr[d, c], state.astype(BF16), preferred_element_type=F32)
        r0 = pl.multiple_of(c * L, L)
        o_scr[pl.ds(r0, L), :] += r[L2:]
        return state * gl_scr[c, d:d + 1, :] + (b_scr[d, c] - r[:L2])

    def body(n, carry):
        s_up, s_dn = carry
        s_up = advance(n, 0, s_up)
        s_dn = advance(_backward_chunk(n, n_chunks, n_ctx_chunks), 1, s_dn)
        return s_up, s_dn

    zero = jnp.zeros((HEAD_DIM, HEAD_DIM), F32)
    lax.fori_loop(0, n_chunks, body, (zero, zero))

    rows = group * L

    def epilogue(n, _):
        r0 = pl.multiple_of(n * rows, rows)
        y = _rms(o_scr[pl.ds(r0, rows), :]) * g_ref[...]
        y_ref[0, pl.ds(r0, rows), :] = (y * _silu(z_ref[0, pl.ds(r0, rows), :])).astype(y_ref.dtype)
        return 0

    lax.fori_loop(0, n_chunks // group, epilogue, 0)


def _gdn(p, conv_w, gates, onorm_g, n_ctx, n_heads, col_qkv, col_z):
    b, s, _ = p.shape
    nc = s // CHUNK
    taps = conv_w.shape[0]
    kern = functools.partial(_gdn_kernel, n_chunks=nc, n_ctx_chunks=n_ctx // CHUNK, n_heads=n_heads,
                             group=_group(nc))
    seq = lambda off: pl.BlockSpec((1, s, HEAD_DIM), lambda bb, h: (bb, 0, off + h))
    wsp = lambda off: pl.BlockSpec((taps, HEAD_DIM), lambda bb, h: (0, off + h))
    qb = col_qkv // HEAD_DIM
    return pl.pallas_call(
        kern,
        grid=(b, n_heads),
        in_specs=[seq(qb), seq(qb + n_heads), seq(qb + 2 * n_heads),
                  wsp(0), wsp(n_heads), wsp(2 * n_heads),
                  pl.BlockSpec((1, s, HEAD_DIM), lambda bb, h: (bb, 0, 0)),
                  seq(col_z // HEAD_DIM),
                  pl.BlockSpec((1, HEAD_DIM), lambda bb, h: (0, 0))],
        out_specs=seq(0),
        out_shape=jax.ShapeDtypeStruct((b, s, n_heads * HEAD_DIM), BF16),
        scratch_shapes=[pltpu.VMEM((s, HEAD_DIM), F32),
                        pltpu.VMEM((2, nc, 3 * CHUNK, HEAD_DIM), BF16),
                        pltpu.VMEM((2, nc, 2 * CHUNK, HEAD_DIM), F32),
                        pltpu.VMEM((nc, SUBLANES, HEAD_DIM), F32)],
        compiler_params=_cparams(("parallel", "arbitrary")),
        name="gated_deltanet",
    )(p, p, p, conv_w, conv_w, conv_w, gates, p, onorm_g.reshape(1, HEAD_DIM))


def _ret_kernel(lg_ref, q_ref, k_ref, v_ref, z_ref, c2_ref, s2_ref, g_ref, y_ref,
                o_scr, q_scr, kzt_scr, *, n_chunks, n_ctx_chunks, group):
    h = pl.program_id(1)
    L = CHUNK
    L2 = 2 * L
    ii = lax.broadcasted_iota(jnp.int32, (L, L), 0)
    jj = lax.broadcasted_iota(jnp.int32, (L, L), 1)
    row_i = lax.broadcasted_iota(jnp.int32, (L, HEAD_DIM), 0).astype(F32)
    row2 = lax.broadcasted_iota(jnp.int32, (L2, HEAD_DIM), 0).astype(F32)
    dist = (ii - jj).astype(F32)
    lgs = (lg_ref[0, h], lg_ref[1, h])
    dsum = (jnp.where(ii >= jj, jnp.exp(jnp.maximum(dist, 0.0) * lgs[0]), 0.0)
            + jnp.where(ii <= jj, jnp.exp(jnp.maximum(-dist, 0.0) * lgs[1]), 0.0))
    xi = (jnp.exp((row_i + 1.0) * lgs[0]), jnp.exp((L - row_i) * lgs[1]))
    zeta2 = jnp.where(row2 < L, jnp.exp((L - 1.0 - row2) * lgs[0]), jnp.exp((row2 - L) * lgs[1]))
    gch = tuple(jnp.exp(jnp.full((1, HEAD_DIM), float(L), F32) * lg) for lg in lgs)
    zeros = jnp.zeros((L, HEAD_DIM), F32)

    def prepare_group(i, _):
        cs = [i * group + g for g in range(group)]
        rows = [pl.ds(pl.multiple_of(c * L, L), L) for c in cs]
        q = _each(lambda r: _rope(q_ref[0, r, :], c2_ref[r, :], s2_ref[r, :]).astype(BF16), rows)
        k = _each(lambda r: _rope(k_ref[0, r, :], c2_ref[r, :], s2_ref[r, :]) * (HEAD_DIM ** -0.5), rows)
        sc = _each(lambda qq, kk: _dot_nt(qq, kk) * dsum, q, k)
        intra = _each(lambda x, r: _dot(x, v_ref[0, r, :]), sc, rows)
        kzt = _each(lambda kk: (jnp.concatenate([kk, kk], axis=0) * zeta2).T.astype(BF16), k)
        for g, c in enumerate(cs):
            q_scr[rows[g], :] = q[g]
            o_scr[rows[g], :] = intra[g]
            kzt_scr[c] = kzt[g]
        return 0

    lax.fori_loop(0, n_chunks // group, prepare_group, 0)

    def advance(c, d, state):
        r0 = pl.multiple_of(c * L, L)
        v = v_ref[0, pl.ds(r0, L), :]
        vpad = jnp.concatenate([v, zeros] if d == 0 else [zeros, v], axis=0).astype(BF16)
        o_scr[pl.ds(r0, L), :] += xi[d] * jnp.dot(q_scr[pl.ds(r0, L), :], state.astype(BF16),
                                                  preferred_element_type=F32)
        return state * gch[d] + jnp.dot(kzt_scr[c], vpad, preferred_element_type=F32)

    def body(n, carry):
        s_up, s_dn = carry
        s_up = advance(n, 0, s_up)
        s_dn = advance(_backward_chunk(n, n_chunks, n_ctx_chunks), 1, s_dn)
        return s_up, s_dn

    zero = jnp.zeros((HEAD_DIM, HEAD_DIM), F32)
    lax.fori_loop(0, n_chunks, body, (zero, zero), unroll=2)

    rows = group * L

    def epilogue(n, _):
        r0 = pl.multiple_of(n * rows, rows)
        o = o_scr[pl.ds(r0, rows), :]
        mu = jnp.mean(o, axis=-1, keepdims=True)
        var = jnp.mean(jnp.square(o - mu), axis=-1, keepdims=True)
        y = (o - mu) * lax.rsqrt(var + EPS) * g_ref[...]
        y_ref[0, pl.ds(r0, rows), :] = (y * _silu(z_ref[0, pl.ds(r0, rows), :])).astype(y_ref.dtype)
        return 0

    lax.fori_loop(0, n_chunks // group, epilogue, 0)


def _retention(lg, p, c2, s2, gnorm_g, n_ctx, n_heads, col_q, col_k, col_v, col_z):
    b, s, _ = p.shape
    nc = s // CHUNK
    kern = functools.partial(_ret_kernel, n_chunks=nc, n_ctx_chunks=n_ctx // CHUNK, group=_group(nc))
    seq = lambda off: pl.BlockSpec((1, s, HEAD_DIM), lambda bb, h: (bb, 0, off // HEAD_DIM + h))
    tab = pl.BlockSpec((s, HEAD_DIM), lambda bb, h: (0, 0))
    return pl.pallas_call(
        kern,
        grid=(b, n_heads),
        in_specs=[pl.BlockSpec(memory_space=pltpu.SMEM), seq(col_q), seq(col_k), seq(col_v), seq(col_z),
                  tab, tab, pl.BlockSpec((1, HEAD_DIM), lambda bb, h: (0, h))],
        out_specs=seq(0),
        out_shape=jax.ShapeDtypeStruct((b, s, n_heads * HEAD_DIM), BF16),
        scratch_shapes=[pltpu.VMEM((s, HEAD_DIM), F32),
                        pltpu.VMEM((s, HEAD_DIM), BF16),
                        pltpu.VMEM((nc, HEAD_DIM, 2 * CHUNK), BF16)],
        compiler_params=_cparams(("parallel", "arbitrary")),
        name="retention",
    )(lg, p, p, p, p, c2, s2, gnorm_g.reshape(1, n_heads * HEAD_DIM))


def _merge_kernel(ya_ref, yd_ref, yr_ref, wa_ref, wd_ref, wr_ref, ga_ref, gd_ref, gr_ref, o_ref):
    acc = _sigmoid(ga_ref[0]) * jnp.dot(ya_ref[0], wa_ref[0], preferred_element_type=F32)
    acc = acc + _sigmoid(gd_ref[0]) * jnp.dot(yd_ref[0], wd_ref[0], preferred_element_type=F32)
    acc = acc + _sigmoid(gr_ref[0]) * jnp.dot(yr_ref[0], wr_ref[0], preferred_element_type=F32)
    o_ref[0] = acc.astype(o_ref.dtype)


def _merge(ya, yd, yr, w_branch, p, col_gates, tm, tn=512):
    b, s, bw = ya.shape
    d = w_branch.shape[2]
    gb = col_gates // tn
    ysp = pl.BlockSpec((1, tm, bw), lambda bb, i, j: (bb, i, 0))
    wsp = lambda br: pl.BlockSpec((1, bw, tn), lambda bb, i, j: (br, 0, j))
    gsp = lambda br: pl.BlockSpec((1, tm, tn), lambda bb, i, j: (bb, i, gb + br * (d // tn) + j))
    return pl.pallas_call(
        _merge_kernel,
        grid=(b, s // tm, d // tn),
        in_specs=[ysp, ysp, ysp, wsp(0), wsp(1), wsp(2), gsp(0), gsp(1), gsp(2)],
        out_specs=pl.BlockSpec((1, tm, tn), lambda bb, i, j: (bb, i, j)),
        out_shape=jax.ShapeDtypeStruct((b, s, d), BF16),
        compiler_params=_cparams(("parallel", "parallel", "arbitrary")),
        name="branch_merge",
    )(ya, yd, yr, w_branch, w_branch, w_branch, p, p, p)


def _outproj_kernel(m_ref, w_ref, x_ref, lat_ref, ctx_ref, o_ref, *, n_ctx, tm):
    i = pl.program_id(1)
    row = i * tm + lax.broadcasted_iota(jnp.int32, (tm, 1), 0)
    gate = jnp.where(row < n_ctx, ctx_ref[...], lat_ref[0])
    o_ref[0] = x_ref[0] + gate * jnp.dot(m_ref[0], w_ref[...], preferred_element_type=F32)


def _out_projection(merged, w_out, xs, gate_lat, gate_ctx, n_ctx, tm, tn=512):
    b, s, d = xs.shape
    kern = functools.partial(_outproj_kernel, n_ctx=n_ctx, tm=tm)
    return pl.pallas_call(
        kern,
        grid=(b, s // tm, d // tn),
        in_specs=[pl.BlockSpec((1, tm, d), lambda bb, i, j: (bb, i, 0)),
                  pl.BlockSpec((d, tn), lambda bb, i, j: (0, j)),
                  pl.BlockSpec((1, tm, tn), lambda bb, i, j: (bb, i, j)),
                  pl.BlockSpec((1, 1, tn), lambda bb, i, j: (bb, 0, j)),
                  pl.BlockSpec((1, tn), lambda bb, i, j: (0, j))],
        out_specs=pl.BlockSpec((1, tm, tn), lambda bb, i, j: (bb, i, j)),
        out_shape=jax.ShapeDtypeStruct((b, s, d), F32),
        compiler_params=_cparams(("parallel", "parallel", "arbitrary")),
        name="out_projection",
    )(merged, w_out, xs, gate_lat, gate_ctx)


def _deinterleave(w, n_heads):
    lead = w.shape[:-1]
    w = w.reshape(*lead, n_heads, HALF, 2)
    return jnp.swapaxes(w, -1, -2).reshape(*lead, n_heads * HEAD_DIM)


class _Cols:
    def __init__(self, d_model):
        aw = d_model // 2
        kvw = aw // ATT_GROUPS
        nh = aw // HEAD_DIM
        self.src = dict(att_q=aw, att_k=kvw, att_v=kvw, att_z=aw, dn_qkv=3 * aw, dn_z=aw,
                        dn_b=2 * nh, dn_a=2 * nh, ret_q=aw, ret_k=aw, ret_v=aw, ret_z=aw,
                        gates=N_BRANCH * d_model)
        order = ("att_q", "att_k", "att_v", "att_z", "dn_qkv", "dn_z", "ret_q", "ret_k", "ret_v",
                 "ret_z", "gates", "dn_b", "dn_a")
        self.order = order
        self.off = {}
        o = 0
        for name in order:
            self.off[name] = o
            o += self.src[name]
        self.used = o
        self.n_heads = nh


def _prep_w_in(w_in, cols, tn):
    offs, o = {}, 0
    for name, width in cols.src.items():
        offs[name] = o
        o += width
    parts = []
    for name in cols.order:
        seg = w_in[:, offs[name]:offs[name] + cols.src[name]]
        if name in ("att_q", "att_k", "ret_q", "ret_k"):
            seg = _deinterleave(seg, cols.src[name] // HEAD_DIM)
        parts.append(seg)
    pad = (-cols.used) % tn
    if pad:
        parts.append(jnp.zeros((w_in.shape[0], pad), w_in.dtype))
    return jnp.concatenate(parts, axis=1).astype(BF16)


def _rope_tables(n_ctx, t):
    rows = t // GRID_W
    row, col = jnp.meshgrid(jnp.arange(rows, dtype=F32), jnp.arange(GRID_W, dtype=F32), indexing="ij")
    n_freq = HEAD_DIM // 4
    inv = ROPE_THETA ** (-jnp.arange(n_freq, dtype=F32) / n_freq)
    ang = jnp.concatenate([row.reshape(-1, 1) * inv, col.reshape(-1, 1) * inv], axis=-1)
    cos, sin = jnp.cos(ang), jnp.sin(ang)
    cos = jnp.concatenate([jnp.ones((n_ctx, HALF), F32), cos], axis=0)
    sin = jnp.concatenate([jnp.zeros((n_ctx, HALF), F32), sin], axis=0)
    return jnp.concatenate([cos, cos], axis=-1), jnp.concatenate([-sin, sin], axis=-1)


def _row_tile(s):
    for tm in (1088, 1024, 544, 512, 256, 128):
        if s % tm == 0:
            return tm
    return s


def kernel(x, c, ctx, c_ctx, norm_g, w_mod, b_mod, w_in, att_qnorm_g, att_knorm_g, dn_conv_w,
           dn_a_log, dn_dt_bias, dn_onorm_g, ret_gnorm_g, w_branch, w_out):
    b, t, d = x.shape
    n_ctx = ctx.shape[1]
    depth = w_in.shape[0]
    s = n_ctx + t
    cols = _Cols(d)
    nh = cols.n_heads
    tn = 512
    tm = _row_tile(s)

    xs = jnp.concatenate([ctx, x], axis=1)
    c2, s2 = _rope_tables(n_ctx, t)

    rows = -(-(b + 1) // SUBLANES) * SUBLANES
    cc = jnp.zeros((rows, d), F32).at[:b].set(c).at[b].set(c_ctx)
    mod = _modulation(cc, w_mod, b_mod)

    decay = jnp.exp(jnp.linspace(math.log(1.0 / 32), math.log(1.0 / 512), nh, dtype=F32))
    log_gamma = jnp.log1p(-decay)
    lg = jnp.stack([log_gamma, log_gamma[::-1]])

    pad_l = jnp.zeros((2 * nh,), F32)
    pad_r = jnp.zeros((HEAD_DIM - 4 * nh,), F32)
    off = cols.off
    for l in range(depth):
        m3 = mod[l].reshape(rows, 3, d)
        mod_lat, mod_ctx = m3[:b], m3[b]
        w = _prep_w_in(w_in[l], cols, tn)
        p = _in_projection(xs, norm_g[l], mod_lat, mod_ctx, w, n_ctx, tm, tn)

        ya = _attention(p, c2, s2, _deinterleave(att_qnorm_g[l], 1), _deinterleave(att_knorm_g[l], 1),
                        n_ctx, nh, off["att_q"], off["att_k"], off["att_v"], off["att_z"])

        nega_lane = jnp.concatenate([pad_l, -jnp.exp(dn_a_log[l].astype(F32)).reshape(-1), pad_r])
        dtb_lane = jnp.concatenate([pad_l, dn_dt_bias[l].astype(F32).reshape(-1), pad_r])
        gates = _gdn_gates(p, nega_lane.reshape(1, HEAD_DIM), dtb_lane.reshape(1, HEAD_DIM), off["dn_b"], nh)
        yd = _gdn(p, dn_conv_w[l].astype(F32), gates, dn_onorm_g[l], n_ctx, nh, off["dn_qkv"], off["dn_z"])

        yr = _retention(lg, p, c2, s2, ret_gnorm_g[l], n_ctx, nh,
                        off["ret_q"], off["ret_k"], off["ret_v"], off["ret_z"])

        merged = _merge(ya, yd, yr, w_branch[l].astype(BF16), p, off["gates"], tm, tn)
        xs = _out_projection(merged, w_out[l].astype(BF16), xs, mod_lat[:, 2:3, :], mod_ctx[2:3, :],
                             n_ctx, tm, tn)

    return xs[:, n_ctx:, :]
```

```python
import functools
import math

import jax
import jax.numpy as jnp
from jax import lax
from jax.experimental import pallas as pl
from jax.experimental.pallas import tpu as pltpu

F32 = jnp.float32
BF16 = jnp.bfloat16

HEAD_DIM = 128
HALF = HEAD_DIM // 2
GRID_W = 64
ATT_GROUPS = 4
ATT_STACK = 2
CHUNK = 64
SUBLANES = 8
BF16_ROWS = 16
COL_TILE = 1024
GDN_PREP_GROUP = 8
NORM_SPLIT = 4
ROPE_THETA = 10000.0
EPS = 1e-6
N_BRANCH = 3
V7X_VMEM_LIMIT = 56 * 1024 * 1024


def _sigmoid(x):
    return 1.0 / (1.0 + jnp.exp(-x))


def _silu(x):
    return x * _sigmoid(x)


def _softplus(x):
    return jnp.maximum(x, 0.0) + jnp.log1p(jnp.exp(-jnp.abs(x)))


def _dot(a, b):
    return jnp.dot(a.astype(BF16), b.astype(BF16), preferred_element_type=F32)


def _dot_nt(a, b):
    return lax.dot_general(a.astype(BF16), b.astype(BF16), (((1,), (1,)), ((), ())),
                           preferred_element_type=F32)


def _split3(x):
    hi = x.astype(BF16)
    r = x - hi.astype(F32)
    mid = r.astype(BF16)
    lo = (r - mid.astype(F32)).astype(BF16)
    return hi, mid, lo


def _dot_sel_l(sel, x):
    hi, mid, lo = _split3(x)
    d = lambda v: jnp.dot(sel, v, preferred_element_type=F32)
    return d(hi) + (d(mid) + d(lo))


def _rope(x, c2, s2):
    return x * c2 + pltpu.roll(x, HALF, 1) * s2


def _rms(x):
    return x * lax.rsqrt(jnp.mean(x * x, axis=-1, keepdims=True) + EPS)


def _cparams(sem):
    return pltpu.CompilerParams(dimension_semantics=sem, vmem_limit_bytes=V7X_VMEM_LIMIT)


def _group(n_chunks):
    for g in (4, 2):
        if n_chunks % g == 0:
            return g
    return 1


def _mod_kernel(c_ref, w_ref, b_ref, o_ref):
    a = _silu(c_ref[...])
    o_ref[0] = _dot(a, w_ref[0]) + b_ref[0]


def _modulation(cc, w_mod, b_mod, tn=512):
    depth, d, n = w_mod.shape
    rows = cc.shape[0]
    return pl.pallas_call(
        _mod_kernel,
        grid=(depth, n // tn),
        in_specs=[pl.BlockSpec((rows, d), lambda l, j: (0, 0)),
                  pl.BlockSpec((1, d, tn), lambda l, j: (l, 0, j)),
                  pl.BlockSpec((1, 1, tn), lambda l, j: (l, 0, j))],
        out_specs=pl.BlockSpec((1, rows, tn), lambda l, j: (l, 0, j)),
        out_shape=jax.ShapeDtypeStruct((depth, rows, n), F32),
        compiler_params=_cparams(("parallel", "parallel")),
        name="modulation",
    )(cc, w_mod, b_mod.reshape(depth, 1, n))


def _inproj_kernel(x_ref, g_ref, lat_ref, ctx_ref, w_ref, o_ref, h_scr, *, n_ctx, tm, d):
    i = pl.program_id(1)
    j = pl.program_id(2)

    @pl.when(j == 0)
    def _():
        rb = tm // NORM_SPLIT

        def norm_rows(n, _):
            r0 = pl.multiple_of(n * rb, rb)
            y = _rms(x_ref[0, pl.ds(r0, rb), :]) * g_ref[...]
            row = i * tm + r0 + lax.broadcasted_iota(jnp.int32, (rb, 1), 0)
            is_ctx = row < n_ctx
            shift = jnp.where(is_ctx, ctx_ref[0:1, :], lat_ref[0, 0:1, :])
            scale = jnp.where(is_ctx, ctx_ref[1:2, :], lat_ref[0, 1:2, :])
            h_scr[pl.ds(r0, rb), :] = (y * (1.0 + scale) + shift).astype(BF16)
            return 0

        lax.fori_loop(0, NORM_SPLIT, norm_rows, 0)

    o_ref[0] = jnp.dot(h_scr[...], w_ref[...], preferred_element_type=F32).astype(o_ref.dtype)


def _in_projection(xs, norm_g, mod_lat, mod_ctx, w, n_ctx, tm, tn=COL_TILE):
    b, s, d = xs.shape
    n = w.shape[1]
    kern = functools.partial(_inproj_kernel, n_ctx=n_ctx, tm=tm, d=d)
    return pl.pallas_call(
        kern,
        grid=(b, s // tm, n // tn),
        in_specs=[pl.BlockSpec((1, tm, d), lambda bb, i, j: (bb, i, 0)),
                  pl.BlockSpec((1, d), lambda bb, i, j: (0, 0)),
                  pl.BlockSpec((1, 3, d), lambda bb, i, j: (bb, 0, 0)),
                  pl.BlockSpec((3, d), lambda bb, i, j: (0, 0)),
                  pl.BlockSpec((d, tn), lambda bb, i, j: (0, j))],
        out_specs=pl.BlockSpec((1, tm, tn), lambda bb, i, j: (bb, i, j)),
        out_shape=jax.ShapeDtypeStruct((b, s, n), BF16),
        scratch_shapes=[pltpu.VMEM((tm, d), BF16)],
        compiler_params=_cparams(("parallel", "parallel", "arbitrary")),
        name="in_projection",
    )(xs, norm_g.reshape(1, d), mod_lat, mod_ctx, w)


def _attn_kernel(q_ref, k_ref, v_ref, z_ref, c2q_ref, s2q_ref, c2_ref, s2_ref, qn_ref, kn_ref,
                 y_ref, k_scr, v_scr, *, n_ctx, tq, blk):
    i = pl.program_id(2)
    s = k_ref.shape[1]

    @pl.when(i == 0)
    def _():
        def prep(n, _):
            r0 = pl.multiple_of(n * blk, blk)
            k = _rms(k_ref[0, pl.ds(r0, blk), :].astype(F32)) * kn_ref[...]
            k = _rope(k, c2_ref[pl.ds(r0, blk), :], s2_ref[pl.ds(r0, blk), :])
            k_scr[pl.ds(r0, blk), :] = k.astype(BF16)
            v_scr[pl.ds(r0, blk), :HEAD_DIM] = v_ref[0, pl.ds(r0, blk), :]
            v_scr[pl.ds(r0, blk), HEAD_DIM:] = jnp.ones((blk, HEAD_DIM), BF16)
            return 0

        lax.fori_loop(0, s // blk, prep, 0)

    def query(g):
        q = q_ref[0, :, g * HEAD_DIM:(g + 1) * HEAD_DIM].astype(F32)
        q = _rope(_rms(q) * qn_ref[...], c2q_ref[...], s2q_ref[...])
        return (q * (HEAD_DIM ** -0.5)).astype(BF16)

    qs = _each(query, list(range(ATT_GROUPS)))
    stacks = [jnp.concatenate(qs[g:g + ATT_STACK], axis=0) for g in range(0, ATT_GROUPS, ATT_STACK)]

    def attend(kv_len):
        k = k_scr[:kv_len, :]
        v1 = v_scr[:kv_len, :]
        sc = _each(lambda q: lax.dot_general(q, k, (((1,), (1,)), ((), ())),
                                             preferred_element_type=F32), stacks)
        m = _each(lambda x: jnp.max(x, axis=-1, keepdims=True), sc)
        p = _each(lambda x, mx: jnp.exp(x - mx).astype(BF16), sc, m)
        ol = _each(lambda x: jnp.dot(x, v1, preferred_element_type=F32), p)
        o = _each(lambda x: x[:, :HEAD_DIM] / x[:, HEAD_DIM:HEAD_DIM + 1], ol)
        heads = [x[g * tq:(g + 1) * tq] for x in o for g in range(ATT_STACK)]
        y = jnp.concatenate(heads, axis=1) * _silu(z_ref[0].astype(F32))
        y_ref[0] = y.astype(y_ref.dtype)

    @pl.when(i < n_ctx // tq)
    def _():
        attend(n_ctx)

    @pl.when(i >= n_ctx // tq)
    def _():
        attend(s)


def _attention(p, c2, s2, qn, kn, n_ctx, n_heads, col_q, col_k, col_v, col_z, tq=256):
    b, s, _ = p.shape
    gw = ATT_GROUPS * HEAD_DIM
    kern = functools.partial(_attn_kernel, n_ctx=n_ctx, tq=tq, blk=n_ctx)
    row_blk = lambda col: pl.BlockSpec((1, tq, gw), lambda bb, h, i: (bb, i, col // gw + h))
    kv_blk = lambda col: pl.BlockSpec((1, s, HEAD_DIM), lambda bb, h, i: (bb, 0, col // HEAD_DIM + h))
    tab_q = pl.BlockSpec((tq, HEAD_DIM), lambda bb, h, i: (i, 0))
    tab = pl.BlockSpec((s, HEAD_DIM), lambda bb, h, i: (0, 0))
    vec = pl.BlockSpec((1, HEAD_DIM), lambda bb, h, i: (0, 0))
    return pl.pallas_call(
        kern,
        grid=(b, n_heads // ATT_GROUPS, s // tq),
        in_specs=[row_blk(col_q), kv_blk(col_k), kv_blk(col_v), row_blk(col_z),
                  tab_q, tab_q, tab, tab, vec, vec],
        out_specs=row_blk(0),
        out_shape=jax.ShapeDtypeStruct((b, s, n_heads * HEAD_DIM), BF16),
        scratch_shapes=[pltpu.VMEM((s, HEAD_DIM), BF16), pltpu.VMEM((s, 2 * HEAD_DIM), BF16)],
        compiler_params=_cparams(("parallel", "parallel", "arbitrary")),
        name="attention",
    )(p, p, p, p, c2, s2, c2, s2, qn.reshape(1, HEAD_DIM), kn.reshape(1, HEAD_DIM))


def _backward_chunk(n, n_chunks, n_ctx_chunks):
    return jnp.where(n < n_ctx_chunks, n_ctx_chunks - 1 - n, n_chunks + n_ctx_chunks - 1 - n)


def _gdn_gates_kernel(raw_ref, nega_ref, dtb_ref, o_ref, *, n_chunks, n_heads):
    L = CHUNK
    ii = lax.broadcasted_iota(jnp.int32, (L, L), 0)
    jj = lax.broadcasted_iota(jnp.int32, (L, L), 1)
    lower_b = jnp.where(ii >= jj, 1.0, 0.0).astype(BF16)
    upper_b = jnp.where(ii <= jj, 1.0, 0.0).astype(BF16)
    lane = lax.broadcasted_iota(jnp.int32, (L, HEAD_DIM), 1)

    def body(c, _):
        r0 = pl.multiple_of(c * L, L)
        raw = raw_ref[0, pl.ds(r0, L), :].astype(F32)
        g = nega_ref[...] * _softplus(raw + dtb_ref[...])
        prefix = _dot_sel_l(lower_b, g)
        suffix = _dot_sel_l(upper_b, g)
        o_ref[0, pl.ds(r0, L), :] = jnp.where(lane < 2 * n_heads, _sigmoid(raw),
                                              jnp.where(lane < 3 * n_heads, prefix, suffix))
        return 0

    lax.fori_loop(0, n_chunks, body, 0)


def _gdn_gates(p, nega_lane, dtb_lane, col_ba, n_heads):
    b, s, _ = p.shape
    kern = functools.partial(_gdn_gates_kernel, n_chunks=s // CHUNK, n_heads=n_heads)
    return pl.pallas_call(
        kern,
        grid=(b,),
        in_specs=[pl.BlockSpec((1, s, HEAD_DIM), lambda bb: (bb, 0, col_ba // HEAD_DIM)),
                  pl.BlockSpec((1, HEAD_DIM), lambda bb: (0, 0)),
                  pl.BlockSpec((1, HEAD_DIM), lambda bb: (0, 0))],
        out_specs=pl.BlockSpec((1, s, HEAD_DIM), lambda bb: (bb, 0, 0)),
        out_shape=jax.ShapeDtypeStruct((b, s, HEAD_DIM), F32),
        compiler_params=_cparams(("parallel",)),
        name="gdn_gates",
    )(p, nega_lane, dtb_lane)


def _each(f, *lists):
    return [f(*a) for a in zip(*lists)]


def _unit_tri_inverse_minus_eye(ms, blk8, level_masks):
    nm = _each(lambda m: jnp.where(blk8, -m, 0.0), ms)
    n2 = _each(lambda a: _dot(a, a), nm)
    n4 = _each(lambda a: _dot(a, a), n2)
    e = _each(lambda a, a2: a + a2 + _dot(a, a2), nm, n2)
    e = _each(lambda x, a4: x + a4 + _dot(x, a4), e, n4)
    for cm in level_masks:
        c = _each(lambda m: jnp.where(cm, m, 0.0), ms)
        tc = _each(lambda x, y: y + _dot(x, y), e, c)
        e = _each(lambda x, y: x - (y + _dot(y, x)), e, tc)
    return e


def _gdn_kernel(q_ref, k_ref, v_ref, wq_ref, wk_ref, wv_ref, gate_ref, z_ref, g_ref, y_ref,
                o_scr, aq_scr, b_scr, gl_scr, *, n_chunks, n_ctx_chunks, n_heads, group):
    h = pl.program_id(1)
    L = CHUNK
    L2 = 2 * L
    s_rows = q_ref.shape[1]
    ii = lax.broadcasted_iota(jnp.int32, (L2, L2), 0)
    jj = lax.broadcasted_iota(jnp.int32, (L2, L2), 1)
    lane = lax.broadcasted_iota(jnp.int32, (L, HEAD_DIM), 1)
    up = ii < L
    down = jnp.logical_not(up)
    same64 = (ii >> 6) == (jj >> 6)
    same32 = (ii >> 5) == (jj >> 5)
    same16 = (ii >> 4) == (jj >> 4)
    same8 = (ii >> 3) == (jj >> 3)
    incl = same64 & ((up & (ii >= jj)) | (down & (ii <= jj)))
    strict = same64 & ((up & (ii > jj)) | (down & (ii < jj)))
    levels = (same16 & jnp.logical_not(same8), same32 & jnp.logical_not(same16),
              same64 & jnp.logical_not(same32))
    row8 = lax.broadcasted_iota(jnp.int32, (SUBLANES, HEAD_DIM), 0)
    zeros_wu = jnp.zeros((L, 2 * HEAD_DIM), BF16)
    taps = wq_ref.shape[0]
    halo = taps // 2

    def pick(gates, idx):
        return jnp.sum(jnp.where(lane == idx, gates, 0.0), axis=1, keepdims=True)

    def conv_silu(ref, w_ref, c):
        r0 = pl.multiple_of(c * L, L)
        x = ref[0, pl.ds(r0, L), :].astype(F32)
        lo = pl.multiple_of(jnp.maximum(r0 - BF16_ROWS, 0), BF16_ROWS)
        hi = pl.multiple_of(jnp.minimum(r0 + L, s_rows - BF16_ROWS), BF16_ROWS)
        seg_first = (c == 0) | (c == n_ctx_chunks)
        seg_last = (c == n_ctx_chunks - 1) | (c == n_chunks - 1)
        prev = ref[0, pl.ds(lo, BF16_ROWS), :].astype(F32) * jnp.where(seg_first, 0.0, 1.0)
        nxt = ref[0, pl.ds(hi, BF16_ROWS), :].astype(F32) * jnp.where(seg_last, 0.0, 1.0)
        ext = jnp.concatenate([prev, x, nxt], axis=0)
        w = w_ref[...]
        first = BF16_ROWS - halo
        acc = ext[first:first + L] * w[0:1]
        for j in range(1, taps):
            acc = acc + ext[first + j:first + j + L] * w[j:j + 1]
        return _silu(acc)

    def l2(x):
        return x * lax.rsqrt(jnp.sum(x * x, axis=-1, keepdims=True) + EPS)

    def stack2(x):
        return jnp.concatenate([x, x], axis=0)

    def prepare_chunks(cs):
        q = _each(lambda c: stack2(l2(conv_silu(q_ref, wq_ref, c)) * (HEAD_DIM ** -0.5)), cs)
        k2 = _each(lambda c: stack2(l2(conv_silu(k_ref, wk_ref, c))), cs)
        v2 = _each(lambda c: stack2(conv_silu(v_ref, wv_ref, c)), cs)
        gates = _each(lambda c: gate_ref[0, pl.ds(pl.multiple_of(c * L, L), L), :], cs)
        beta2 = _each(lambda g: jnp.concatenate([pick(g, h), pick(g, n_heads + h)], axis=0), gates)
        pf = _each(lambda g: pick(g, 2 * n_heads + h), gates)
        pb = _each(lambda g: pick(g, 3 * n_heads + h), gates)
        p2 = _each(lambda x, y: jnp.concatenate([x, y], axis=0), pf, pb)
        gtot2 = _each(lambda x, y: jnp.concatenate([jnp.broadcast_to(x[L - 1:L], (L, 1)),
                                                    jnp.broadcast_to(y[0:1], (L, 1))], axis=0), pf, pb)
        pcol = _each(lambda x: jnp.broadcast_to(x, (L2, L2)), p2)
        diff = _each(lambda x: x - x.T, pcol)
        decay = _each(lambda x: jnp.where(incl, jnp.exp(jnp.where(incl, x, 0.0)), 0.0), diff)
        a = _each(lambda kk, qq: _dot_nt(jnp.concatenate([kk, qq], axis=0), kk), k2, q)
        m = _each(lambda bt, x, dc: jnp.where(strict, bt * x[:L2] * dc, 0.0), beta2, a, decay)
        qk = _each(lambda x, dc: (x[L2:] * dc).astype(BF16), a, decay)
        e = _unit_tri_inverse_minus_eye(m, same8, levels)
        e_in = _each(jnp.exp, p2)
        rhs = _each(lambda kk, vv, bt, ei: jnp.concatenate([kk * (bt * ei), vv * bt], axis=1),
                    k2, v2, beta2, e_in)
        wu = _each(lambda x, r: (r + _dot(x, r)).astype(BF16), e, rhs)
        kdt = _each(lambda kk, gt, pp: (kk * jnp.exp(gt - pp)).T.astype(BF16), k2, gtot2, p2)
        ab_up = _each(lambda kt, x: jnp.dot(kt, jnp.concatenate([x[:L], zeros_wu], axis=0),
                                            preferred_element_type=F32), kdt, wu)
        ab_dn = _each(lambda kt, x: jnp.dot(kt, jnp.concatenate([zeros_wu, x[L:]], axis=0),
                                            preferred_element_type=F32), kdt, wu)
        qo = _each(lambda x, y: jnp.dot(x, y, preferred_element_type=F32), qk, wu)
        qp = _each(lambda qq, ei, x: qq * ei - x[:, :HEAD_DIM], q, e_in, qo)
        for g, c in enumerate(cs):
            o_scr[pl.ds(pl.multiple_of(c * L, L), L), :] = qo[g][:L, HEAD_DIM:] + qo[g][L:, HEAD_DIM:]
            aq_scr[0, c] = jnp.concatenate([ab_up[g][:, :HEAD_DIM], qp[g][:L]], axis=0).astype(BF16)
            aq_scr[1, c] = jnp.concatenate([ab_dn[g][:, :HEAD_DIM], qp[g][L:]], axis=0).astype(BF16)
            b_scr[0, c] = ab_up[g][:, HEAD_DIM:]
            b_scr[1, c] = ab_dn[g][:, HEAD_DIM:]
            gl_scr[c] = jnp.exp(jnp.where(row8 == 0,
                                          jnp.broadcast_to(pf[g][L - 1:L], (SUBLANES, HEAD_DIM)),
                                          jnp.broadcast_to(pb[g][0:1], (SUBLANES, HEAD_DIM))))

    def prepare_group(i, _):
        prepare_chunks([i * GDN_PREP_GROUP + g for g in range(GDN_PREP_GROUP)])
        return 0

    n_full = n_chunks // GDN_PREP_GROUP
    if n_full:
        lax.fori_loop(0, n_full, prepare_group, 0)
    if n_chunks % GDN_PREP_GROUP:
        prepare_chunks([jnp.int32(c) for c in range(n_full * GDN_PREP_GROUP, n_chunks)])

    def advance(c, d, state):
        r = jnp.dot(aq_scr[d, c], state.astype(BF16), preferred_element_type=F32)
        r0 = pl.multiple_of(c * L, L)
        o_scr[pl.ds(r0, L), :] += r[L2:]
        return state * gl_scr[c, d:d + 1, :] + (b_scr[d, c] - r[:L2])

    def body(n, carry):
        s_up, s_dn = carry
        s_up = advance(n, 0, s_up)
        s_dn = advance(_backward_chunk(n, n_chunks, n_ctx_chunks), 1, s_dn)
        return s_up, s_dn

    zero = jnp.zeros((HEAD_DIM, HEAD_DIM), F32)
    lax.fori_loop(0, n_chunks, body, (zero, zero))

    rows = group * L

    def epilogue(n, _):
        r0 = pl.multiple_of(n * rows, rows)
        y = _rms(o_scr[pl.ds(r0, rows), :]) * g_ref[...]
        z = z_ref[0, pl.ds(r0, rows), :].astype(F32)
        y_ref[0, pl.ds(r0, rows), :] = (y * _silu(z)).astype(y_ref.dtype)
        return 0

    lax.fori_loop(0, n_chunks // group, epilogue, 0)


def _gdn(p, conv_w, gates, onorm_g, n_ctx, n_heads, col_qkv, col_z):
    b, s, _ = p.shape
    nc = s // CHUNK
    taps = conv_w.shape[0]
    kern = functools.partial(_gdn_kernel, n_chunks=nc, n_ctx_chunks=n_ctx // CHUNK, n_heads=n_heads,
                             group=_group(nc))
    seq = lambda off: pl.BlockSpec((1, s, HEAD_DIM), lambda bb, h: (bb, 0, off + h))
    wsp = lambda off: pl.BlockSpec((taps, HEAD_DIM), lambda bb, h: (0, off + h))
    qb = col_qkv // HEAD_DIM
    return pl.pallas_call(
        kern,
        grid=(b, n_heads),
        in_specs=[seq(qb), seq(qb + n_heads), seq(qb + 2 * n_heads),
                  wsp(0), wsp(n_heads), wsp(2 * n_heads),
                  pl.BlockSpec((1, s, HEAD_DIM), lambda bb, h: (bb, 0, 0)),
                  seq(col_z // HEAD_DIM),
                  pl.BlockSpec((1, HEAD_DIM), lambda bb, h: (0, 0))],
        out_specs=seq(0),
        out_shape=jax.ShapeDtypeStruct((b, s, n_heads * HEAD_DIM), BF16),
        scratch_shapes=[pltpu.VMEM((s, HEAD_DIM), F32),
                        pltpu.VMEM((2, nc, 3 * CHUNK, HEAD_DIM), BF16),
                        pltpu.VMEM((2, nc, 2 * CHUNK, HEAD_DIM), F32),
                        pltpu.VMEM((nc, SUBLANES, HEAD_DIM), F32)],
        compiler_params=_cparams(("parallel", "arbitrary")),
        name="gated_deltanet",
    )(p, p, p, conv_w, conv_w, conv_w, gates, p, onorm_g.reshape(1, HEAD_DIM))


def _ret_kernel(lg_ref, q_ref, k_ref, v_ref, z_ref, c2_ref, s2_ref, g_ref, y_ref,
                o_scr, q_scr, kzt_scr, *, n_chunks, n_ctx_chunks, group):
    h = pl.program_id(1)
    L = CHUNK
    L2 = 2 * L
    ii = lax.broadcasted_iota(jnp.int32, (L, L), 0)
    jj = lax.broadcasted_iota(jnp.int32, (L, L), 1)
    row_i = lax.broadcasted_iota(jnp.int32, (L, HEAD_DIM), 0).astype(F32)
    row2 = lax.broadcasted_iota(jnp.int32, (L2, HEAD_DIM), 0).astype(F32)
    dist = (ii - jj).astype(F32)
    lgs = (lg_ref[0, h], lg_ref[1, h])
    dsum = (jnp.where(ii >= jj, jnp.exp(jnp.maximum(dist, 0.0) * lgs[0]), 0.0)
            + jnp.where(ii <= jj, jnp.exp(jnp.maximum(-dist, 0.0) * lgs[1]), 0.0))
    xi = (jnp.exp((row_i + 1.0) * lgs[0]), jnp.exp((L - row_i) * lgs[1]))
    zeta2 = jnp.where(row2 < L, jnp.exp((L - 1.0 - row2) * lgs[0]), jnp.exp((row2 - L) * lgs[1]))
    gch = tuple(jnp.exp(jnp.full((1, HEAD_DIM), float(L), F32) * lg) for lg in lgs)
    zeros = jnp.zeros((L, HEAD_DIM), v_ref.dtype)

    def prepare_group(i, _):
        cs = [i * group + g for g in range(group)]
        rows = [pl.ds(pl.multiple_of(c * L, L), L) for c in cs]
        q = _each(lambda r: _rope(q_ref[0, r, :].astype(F32), c2_ref[r, :], s2_ref[r, :]).astype(BF16), rows)
        k = _each(lambda r: _rope(k_ref[0, r, :].astype(F32), c2_ref[r, :], s2_ref[r, :])
                  * (HEAD_DIM ** -0.5), rows)
        sc = _each(lambda qq, kk: _dot_nt(qq, kk) * dsum, q, k)
        intra = _each(lambda x, r: _dot(x, v_ref[0, r, :]), sc, rows)
        kzt = _each(lambda kk: (jnp.concatenate([kk, kk], axis=0) * zeta2).T.astype(BF16), k)
        for g, c in enumerate(cs):
            q_scr[rows[g], :] = q[g]
            o_scr[rows[g], :] = intra[g]
            kzt_scr[c] = kzt[g]
        return 0

    lax.fori_loop(0, n_chunks // group, prepare_group, 0)

    def advance(c, d, state):
        r0 = pl.multiple_of(c * L, L)
        v = v_ref[0, pl.ds(r0, L), :]
        vpad = jnp.concatenate([v, zeros] if d == 0 else [zeros, v], axis=0)
        o_scr[pl.ds(r0, L), :] += xi[d] * jnp.dot(q_scr[pl.ds(r0, L), :], state.astype(BF16),
                                                  preferred_element_type=F32)
        return state * gch[d] + jnp.dot(kzt_scr[c], vpad, preferred_element_type=F32)

    def body(n, carry):
        s_up, s_dn = carry
        s_up = advance(n, 0, s_up)
        s_dn = advance(_backward_chunk(n, n_chunks, n_ctx_chunks), 1, s_dn)
        return s_up, s_dn

    zero = jnp.zeros((HEAD_DIM, HEAD_DIM), F32)
    lax.fori_loop(0, n_chunks, body, (zero, zero), unroll=2)

    rows = group * L

    def epilogue(n, _):
        r0 = pl.multiple_of(n * rows, rows)
        o = o_scr[pl.ds(r0, rows), :]
        mu = jnp.mean(o, axis=-1, keepdims=True)
        var = jnp.mean(jnp.square(o - mu), axis=-1, keepdims=True)
        y = (o - mu) * lax.rsqrt(var + EPS) * g_ref[...]
        z = z_ref[0, pl.ds(r0, rows), :].astype(F32)
        y_ref[0, pl.ds(r0, rows), :] = (y * _silu(z)).astype(y_ref.dtype)
        return 0

    lax.fori_loop(0, n_chunks // group, epilogue, 0)


def _retention(lg, p, c2, s2, gnorm_g, n_ctx, n_heads, col_q, col_k, col_v, col_z):
    b, s, _ = p.shape
    nc = s // CHUNK
    kern = functools.partial(_ret_kernel, n_chunks=nc, n_ctx_chunks=n_ctx // CHUNK, group=_group(nc))
    seq = lambda off: pl.BlockSpec((1, s, HEAD_DIM), lambda bb, h: (bb, 0, off // HEAD_DIM + h))
    tab = pl.BlockSpec((s, HEAD_DIM), lambda bb, h: (0, 0))
    return pl.pallas_call(
        kern,
        grid=(b, n_heads),
        in_specs=[pl.BlockSpec(memory_space=pltpu.SMEM), seq(col_q), seq(col_k), seq(col_v), seq(col_z),
                  tab, tab, pl.BlockSpec((1, HEAD_DIM), lambda bb, h: (0, h))],
        out_specs=seq(0),
        out_shape=jax.ShapeDtypeStruct((b, s, n_heads * HEAD_DIM), BF16),
        scratch_shapes=[pltpu.VMEM((s, HEAD_DIM), F32),
                        pltpu.VMEM((s, HEAD_DIM), BF16),
                        pltpu.VMEM((nc, HEAD_DIM, 2 * CHUNK), BF16)],
        compiler_params=_cparams(("parallel", "arbitrary")),
        name="retention",
    )(lg, p, p, p, p, c2, s2, gnorm_g.reshape(1, n_heads * HEAD_DIM))


def _merge_kernel(ya_ref, yd_ref, yr_ref, wa_ref, wd_ref, wr_ref, ga_ref, gd_ref, gr_ref, o_ref):
    gate = lambda g_ref: _sigmoid(g_ref[0].astype(F32))
    acc = gate(ga_ref) * jnp.dot(ya_ref[0], wa_ref[0], preferred_element_type=F32)
    acc = acc + gate(gd_ref) * jnp.dot(yd_ref[0], wd_ref[0], preferred_element_type=F32)
    acc = acc + gate(gr_ref) * jnp.dot(yr_ref[0], wr_ref[0], preferred_element_type=F32)
    o_ref[0] = acc.astype(o_ref.dtype)


def _merge(ya, yd, yr, w_branch, p, col_gates, tm, tn=COL_TILE):
    b, s, bw = ya.shape
    d = w_branch.shape[2]
    gb = col_gates // tn
    ysp = pl.BlockSpec((1, tm, bw), lambda bb, i, j: (bb, i, 0))
    wsp = lambda br: pl.BlockSpec((1, bw, tn), lambda bb, i, j: (br, 0, j))
    gsp = lambda br: pl.BlockSpec((1, tm, tn), lambda bb, i, j: (bb, i, gb + br * (d // tn) + j))
    return pl.pallas_call(
        _merge_kernel,
        grid=(b, s // tm, d // tn),
        in_specs=[ysp, ysp, ysp, wsp(0), wsp(1), wsp(2), gsp(0), gsp(1), gsp(2)],
        out_specs=pl.BlockSpec((1, tm, tn), lambda bb, i, j: (bb, i, j)),
        out_shape=jax.ShapeDtypeStruct((b, s, d), BF16),
        compiler_params=_cparams(("parallel", "parallel", "arbitrary")),
        name="branch_merge",
    )(ya, yd, yr, w_branch, w_branch, w_branch, p, p, p)


def _outproj_kernel(m_ref, w_ref, x_ref, lat_ref, ctx_ref, o_ref, *, n_ctx, tm):
    i = pl.program_id(1)
    row = i * tm + lax.broadcasted_iota(jnp.int32, (tm, 1), 0)
    gate = jnp.where(row < n_ctx, ctx_ref[...], lat_ref[0])
    o_ref[0] = x_ref[0] + gate * jnp.dot(m_ref[0], w_ref[...], preferred_element_type=F32)


def _out_projection(merged, w_out, xs, gate_lat, gate_ctx, n_ctx, tm, tn=COL_TILE):
    b, s, d = xs.shape
    kern = functools.partial(_outproj_kernel, n_ctx=n_ctx, tm=tm)
    return pl.pallas_call(
        kern,
        grid=(b, s // tm, d // tn),
        in_specs=[pl.BlockSpec((1, tm, d), lambda bb, i, j: (bb, i, 0)),
                  pl.BlockSpec((d, tn), lambda bb, i, j: (0, j)),
                  pl.BlockSpec((1, tm, tn), lambda bb, i, j: (bb, i, j)),
                  pl.BlockSpec((1, 1, tn), lambda bb, i, j: (bb, 0, j)),
                  pl.BlockSpec((1, tn), lambda bb, i, j: (0, j))],
        out_specs=pl.BlockSpec((1, tm, tn), lambda bb, i, j: (bb, i, j)),
        out_shape=jax.ShapeDtypeStruct((b, s, d), F32),
        compiler_params=_cparams(("parallel", "parallel", "arbitrary")),
        name="out_projection",
    )(merged, w_out, xs, gate_lat, gate_ctx)


def _deinterleave(w, n_heads):
    lead = w.shape[:-1]
    w = w.reshape(*lead, n_heads, HALF, 2)
    return jnp.swapaxes(w, -1, -2).reshape(*lead, n_heads * HEAD_DIM)


class _Cols:
    def __init__(self, d_model):
        aw = d_model // 2
        kvw = aw // ATT_GROUPS
        nh = aw // HEAD_DIM
        self.src = dict(att_q=aw, att_k=kvw, att_v=kvw, att_z=aw, dn_qkv=3 * aw, dn_z=aw,
                        dn_b=2 * nh, dn_a=2 * nh, ret_q=aw, ret_k=aw, ret_v=aw, ret_z=aw,
                        gates=N_BRANCH * d_model)
        self.order = ("att_q", "att_k", "att_v", "att_z", "dn_qkv", "dn_z", "ret_q", "ret_k", "ret_v",
                      "ret_z", "dn_b", "dn_a", "gates")
        self.off = {}
        o = 0
        for name in self.order:
            if name == "gates":
                o = -(-o // COL_TILE) * COL_TILE
            self.off[name] = o
            o += self.src[name]
        self.width = -(-o // COL_TILE) * COL_TILE
        self.n_heads = nh


def _prep_w_in(w_in, cols):
    offs, o = {}, 0
    for name, width in cols.src.items():
        offs[name] = o
        o += width
    parts, filled = [], 0
    for name in cols.order:
        if cols.off[name] > filled:
            parts.append(jnp.zeros((w_in.shape[0], cols.off[name] - filled), w_in.dtype))
        seg = w_in[:, offs[name]:offs[name] + cols.src[name]]
        if name in ("att_q", "att_k", "ret_q", "ret_k"):
            seg = _deinterleave(seg, cols.src[name] // HEAD_DIM)
        parts.append(seg)
        filled = cols.off[name] + cols.src[name]
    if cols.width > filled:
        parts.append(jnp.zeros((w_in.shape[0], cols.width - filled), w_in.dtype))
    return jnp.concatenate(parts, axis=1).astype(BF16)


def _rope_tables(n_ctx, t):
    rows = t // GRID_W
    row, col = jnp.meshgrid(jnp.arange(rows, dtype=F32), jnp.arange(GRID_W, dtype=F32), indexing="ij")
    n_freq = HEAD_DIM // 4
    inv = ROPE_THETA ** (-jnp.arange(n_freq, dtype=F32) / n_freq)
    ang = jnp.concatenate([row.reshape(-1, 1) * inv, col.reshape(-1, 1) * inv], axis=-1)
    cos, sin = jnp.cos(ang), jnp.sin(ang)
    cos = jnp.concatenate([jnp.ones((n_ctx, HALF), F32), cos], axis=0)
    sin = jnp.concatenate([jnp.zeros((n_ctx, HALF), F32), sin], axis=0)
    return jnp.concatenate([cos, cos], axis=-1), jnp.concatenate([-sin, sin], axis=-1)


def _row_tile(s):
    for tm in (1088, 1024, 544, 512, 256, 128):
        if s % tm == 0:
            return tm
    return s


def kernel(x, c, ctx, c_ctx, norm_g, w_mod, b_mod, w_in, att_qnorm_g, att_knorm_g, dn_conv_w,
           dn_a_log, dn_dt_bias, dn_onorm_g, ret_gnorm_g, w_branch, w_out):
    b, t, d = x.shape
    n_ctx = ctx.shape[1]
    depth = w_in.shape[0]
    s = n_ctx + t
    cols = _Cols(d)
    nh = cols.n_heads
    tm = _row_tile(s)

    xs = jnp.concatenate([ctx, x], axis=1)
    c2, s2 = _rope_tables(n_ctx, t)

    rows = -(-(b + 1) // SUBLANES) * SUBLANES
    cc = jnp.zeros((rows, d), F32).at[:b].set(c).at[b].set(c_ctx)
    mod = _modulation(cc, w_mod, b_mod)

    decay = jnp.exp(jnp.linspace(math.log(1.0 / 32), math.log(1.0 / 512), nh, dtype=F32))
    log_gamma = jnp.log1p(-decay)
    lg = jnp.stack([log_gamma, log_gamma[::-1]])

    pad_l = jnp.zeros((2 * nh,), F32)
    pad_r = jnp.zeros((HEAD_DIM - 4 * nh,), F32)
    off = cols.off
    for l in range(depth):
        m3 = mod[l].reshape(rows, 3, d)
        mod_lat, mod_ctx = m3[:b], m3[b]
        w = _prep_w_in(w_in[l], cols)
        p = _in_projection(xs, norm_g[l], mod_lat, mod_ctx, w, n_ctx, tm)

        ya = _attention(p, c2, s2, _deinterleave(att_qnorm_g[l], 1), _deinterleave(att_knorm_g[l], 1),
                        n_ctx, nh, off["att_q"], off["att_k"], off["att_v"], off["att_z"])

        nega_lane = jnp.concatenate([pad_l, -jnp.exp(dn_a_log[l].astype(F32)).reshape(-1), pad_r])
        dtb_lane = jnp.concatenate([pad_l, dn_dt_bias[l].astype(F32).reshape(-1), pad_r])
        gates = _gdn_gates(p, nega_lane.reshape(1, HEAD_DIM), dtb_lane.reshape(1, HEAD_DIM), off["dn_b"], nh)
        yd = _gdn(p, dn_conv_w[l].astype(F32), gates, dn_onorm_g[l], n_ctx, nh, off["dn_qkv"], off["dn_z"])

        yr = _retention(lg, p, c2, s2, ret_gnorm_g[l], n_ctx, nh,
                        off["ret_q"], off["ret_k"], off["ret_v"], off["ret_z"])

        merged = _merge(ya, yd, yr, w_branch[l].astype(BF16), p, off["gates"], tm)
        xs = _out_projection(merged, w_out[l].astype(BF16), xs, mod_lat[:, 2:3, :], mod_ctx[2:3, :],
                             n_ctx, tm)

    return xs[:, n_ctx:, :]
```

```python
import functools
import math

import jax
import jax.numpy as jnp
from jax import lax
from jax.experimental import pallas as pl
from jax.experimental.pallas import tpu as pltpu

F32 = jnp.float32
BF16 = jnp.bfloat16

HEAD_DIM = 128
HALF = HEAD_DIM // 2
GRID_W = 64
ATT_GROUPS = 4
ATT_STACK = 2
CHUNK = 64
SUBLANES = 8
CONV_PAD = 16
COL_TILE = 1024
GDN_PREP_GROUP = 12
NORM_SPLIT = 4
ROPE_THETA = 10000.0
EPS = 1e-6
N_BRANCH = 3
V7X_VMEM_LIMIT = 56 * 1024 * 1024


def _sigmoid(x):
    return 1.0 / (1.0 + jnp.exp(-x))


def _silu(x):
    return x * _sigmoid(x)


def _softplus(x):
    return jnp.maximum(x, 0.0) + jnp.log1p(jnp.exp(-jnp.abs(x)))


def _dot(a, b):
    return jnp.dot(a.astype(BF16), b.astype(BF16), preferred_element_type=F32)


def _dot_nt(a, b):
    return lax.dot_general(a.astype(BF16), b.astype(BF16), (((1,), (1,)), ((), ())),
                           preferred_element_type=F32)


def _split3(x):
    hi = x.astype(BF16)
    r = x - hi.astype(F32)
    mid = r.astype(BF16)
    lo = (r - mid.astype(F32)).astype(BF16)
    return hi, mid, lo


def _dot_sel_l(sel, x):
    hi, mid, lo = _split3(x)
    d = lambda v: jnp.dot(sel, v, preferred_element_type=F32)
    return d(hi) + (d(mid) + d(lo))


def _rope(x, c2, s2):
    return x * c2 + pltpu.roll(x, HALF, 1) * s2


def _rms(x):
    return x * lax.rsqrt(jnp.mean(x * x, axis=-1, keepdims=True) + EPS)


def _cparams(sem):
    return pltpu.CompilerParams(dimension_semantics=sem, vmem_limit_bytes=V7X_VMEM_LIMIT)


def _group(n_chunks):
    for g in (4, 2):
        if n_chunks % g == 0:
            return g
    return 1


def _mod_kernel(c_ref, w_ref, b_ref, o_ref):
    a = _silu(c_ref[...])
    o_ref[0] = _dot(a, w_ref[0]) + b_ref[0]


def _modulation(cc, w_mod, b_mod, tn=512):
    depth, d, n = w_mod.shape
    rows = cc.shape[0]
    return pl.pallas_call(
        _mod_kernel,
        grid=(depth, n // tn),
        in_specs=[pl.BlockSpec((rows, d), lambda l, j: (0, 0)),
                  pl.BlockSpec((1, d, tn), lambda l, j: (l, 0, j)),
                  pl.BlockSpec((1, 1, tn), lambda l, j: (l, 0, j))],
        out_specs=pl.BlockSpec((1, rows, tn), lambda l, j: (l, 0, j)),
        out_shape=jax.ShapeDtypeStruct((depth, rows, n), F32),
        compiler_params=_cparams(("parallel", "parallel")),
        name="modulation",
    )(cc, w_mod, b_mod.reshape(depth, 1, n))


def _inproj_kernel(x_ref, g_ref, lat_ref, ctx_ref, w_ref, o_ref, h_scr, *, n_ctx, tm, d):
    i = pl.program_id(1)
    j = pl.program_id(2)

    @pl.when(j == 0)
    def _():
        rb = tm // NORM_SPLIT

        def norm_rows(n, _):
            r0 = pl.multiple_of(n * rb, rb)
            y = _rms(x_ref[0, pl.ds(r0, rb), :]) * g_ref[...]
            row = i * tm + r0 + lax.broadcasted_iota(jnp.int32, (rb, 1), 0)
            is_ctx = row < n_ctx
            shift = jnp.where(is_ctx, ctx_ref[0:1, :], lat_ref[0, 0:1, :])
            scale = jnp.where(is_ctx, ctx_ref[1:2, :], lat_ref[0, 1:2, :])
            h_scr[pl.ds(r0, rb), :] = (y * (1.0 + scale) + shift).astype(BF16)
            return 0

        lax.fori_loop(0, NORM_SPLIT, norm_rows, 0)

    o_ref[0] = jnp.dot(h_scr[...], w_ref[...], preferred_element_type=F32).astype(o_ref.dtype)


def _in_projection(xs, norm_g, mod_lat, mod_ctx, w, n_ctx, tm, tn=COL_TILE):
    b, s, d = xs.shape
    n = w.shape[1]
    kern = functools.partial(_inproj_kernel, n_ctx=n_ctx, tm=tm, d=d)
    return pl.pallas_call(
        kern,
        grid=(b, s // tm, n // tn),
        in_specs=[pl.BlockSpec((1, tm, d), lambda bb, i, j: (bb, i, 0)),
                  pl.BlockSpec((1, d), lambda bb, i, j: (0, 0)),
                  pl.BlockSpec((1, 3, d), lambda bb, i, j: (bb, 0, 0)),
                  pl.BlockSpec((3, d), lambda bb, i, j: (0, 0)),
                  pl.BlockSpec((d, tn), lambda bb, i, j: (0, j))],
        out_specs=pl.BlockSpec((1, tm, tn), lambda bb, i, j: (bb, i, j)),
        out_shape=jax.ShapeDtypeStruct((b, s, n), BF16),
        scratch_shapes=[pltpu.VMEM((tm, d), BF16)],
        compiler_params=_cparams(("parallel", "parallel", "arbitrary")),
        name="in_projection",
    )(xs, norm_g.reshape(1, d), mod_lat, mod_ctx, w)


def _attn_kernel(q_ref, k_ref, v_ref, z_ref, c2q_ref, s2q_ref, c2_ref, s2_ref, qn_ref, kn_ref,
                 y_ref, k_scr, v_scr, *, n_ctx, tq, blk):
    i = pl.program_id(2)
    s = k_ref.shape[1]

    @pl.when(i == 0)
    def _():
        def prep(n, _):
            r0 = pl.multiple_of(n * blk, blk)
            k = _rms(k_ref[0, pl.ds(r0, blk), :].astype(F32)) * kn_ref[...]
            k = _rope(k, c2_ref[pl.ds(r0, blk), :], s2_ref[pl.ds(r0, blk), :])
            k_scr[pl.ds(r0, blk), :] = k.astype(BF16)
            v_scr[pl.ds(r0, blk), :HEAD_DIM] = v_ref[0, pl.ds(r0, blk), :]
            v_scr[pl.ds(r0, blk), HEAD_DIM:] = jnp.ones((blk, HEAD_DIM), BF16)
            return 0

        lax.fori_loop(0, s // blk, prep, 0)

    def query(g):
        q = q_ref[0, :, g * HEAD_DIM:(g + 1) * HEAD_DIM].astype(F32)
        q = _rope(_rms(q) * qn_ref[...], c2q_ref[...], s2q_ref[...])
        return (q * (HEAD_DIM ** -0.5)).astype(BF16)

    qs = _each(query, list(range(ATT_GROUPS)))
    stacks = [jnp.concatenate(qs[g:g + ATT_STACK], axis=0) for g in range(0, ATT_GROUPS, ATT_STACK)]

    def attend(kv_len):
        k = k_scr[:kv_len, :]
        v1 = v_scr[:kv_len, :]
        sc = _each(lambda q: lax.dot_general(q, k, (((1,), (1,)), ((), ())),
                                             preferred_element_type=F32), stacks)
        m = _each(lambda x: jnp.max(x, axis=-1, keepdims=True), sc)
        p = _each(lambda x, mx: jnp.exp(x - mx).astype(BF16), sc, m)
        ol = _each(lambda x: jnp.dot(x, v1, preferred_element_type=F32), p)
        o = _each(lambda x: x[:, :HEAD_DIM] / x[:, HEAD_DIM:HEAD_DIM + 1], ol)
        heads = [x[g * tq:(g + 1) * tq] for x in o for g in range(ATT_STACK)]
        y = jnp.concatenate(heads, axis=1) * _silu(z_ref[0].astype(F32))
        y_ref[0] = y.astype(y_ref.dtype)

    @pl.when(i < n_ctx // tq)
    def _():
        attend(n_ctx)

    @pl.when(i >= n_ctx // tq)
    def _():
        attend(s)


def _attention(p, c2, s2, qn, kn, n_ctx, n_heads, col_q, col_k, col_v, col_z, tq=256):
    b, s, _ = p.shape
    gw = ATT_GROUPS * HEAD_DIM
    kern = functools.partial(_attn_kernel, n_ctx=n_ctx, tq=tq, blk=n_ctx)
    row_blk = lambda col: pl.BlockSpec((1, tq, gw), lambda bb, h, i: (bb, i, col // gw + h))
    kv_blk = lambda col: pl.BlockSpec((1, s, HEAD_DIM), lambda bb, h, i: (bb, 0, col // HEAD_DIM + h))
    tab_q = pl.BlockSpec((tq, HEAD_DIM), lambda bb, h, i: (i, 0))
    tab = pl.BlockSpec((s, HEAD_DIM), lambda bb, h, i: (0, 0))
    vec = pl.BlockSpec((1, HEAD_DIM), lambda bb, h, i: (0, 0))
    return pl.pallas_call(
        kern,
        grid=(b, n_heads // ATT_GROUPS, s // tq),
        in_specs=[row_blk(col_q), kv_blk(col_k), kv_blk(col_v), row_blk(col_z),
                  tab_q, tab_q, tab, tab, vec, vec],
        out_specs=row_blk(0),
        out_shape=jax.ShapeDtypeStruct((b, s, n_heads * HEAD_DIM), BF16),
        scratch_shapes=[pltpu.VMEM((s, HEAD_DIM), BF16), pltpu.VMEM((s, 2 * HEAD_DIM), BF16)],
        compiler_params=_cparams(("parallel", "parallel", "arbitrary")),
        name="attention",
    )(p, p, p, p, c2, s2, c2, s2, qn.reshape(1, HEAD_DIM), kn.reshape(1, HEAD_DIM))


def _backward_chunk(n, n_chunks, n_ctx_chunks):
    return jnp.where(n < n_ctx_chunks, n_ctx_chunks - 1 - n, n_chunks + n_ctx_chunks - 1 - n)


def _gdn_gates_kernel(raw_ref, nega_ref, dtb_ref, o_ref, *, n_chunks, n_heads):
    L = CHUNK
    ii = lax.broadcasted_iota(jnp.int32, (L, L), 0)
    jj = lax.broadcasted_iota(jnp.int32, (L, L), 1)
    lower_b = jnp.where(ii >= jj, 1.0, 0.0).astype(BF16)
    upper_b = jnp.where(ii <= jj, 1.0, 0.0).astype(BF16)
    lane = lax.broadcasted_iota(jnp.int32, (L, HEAD_DIM), 1)

    def body(c, _):
        r0 = pl.multiple_of(c * L, L)
        raw = raw_ref[0, pl.ds(r0, L), :].astype(F32)
        g = nega_ref[...] * _softplus(raw + dtb_ref[...])
        prefix = _dot_sel_l(lower_b, g)
        suffix = _dot_sel_l(upper_b, g)
        o_ref[0, pl.ds(r0, L), :] = jnp.where(lane < 2 * n_heads, _sigmoid(raw),
                                              jnp.where(lane < 3 * n_heads, prefix, suffix))
        return 0

    lax.fori_loop(0, n_chunks, body, 0)


def _gdn_gates(p, nega_lane, dtb_lane, col_ba, n_heads):
    b, s, _ = p.shape
    kern = functools.partial(_gdn_gates_kernel, n_chunks=s // CHUNK, n_heads=n_heads)
    return pl.pallas_call(
        kern,
        grid=(b,),
        in_specs=[pl.BlockSpec((1, s, HEAD_DIM), lambda bb: (bb, 0, col_ba // HEAD_DIM)),
                  pl.BlockSpec((1, HEAD_DIM), lambda bb: (0, 0)),
                  pl.BlockSpec((1, HEAD_DIM), lambda bb: (0, 0))],
        out_specs=pl.BlockSpec((1, s, HEAD_DIM), lambda bb: (bb, 0, 0)),
        out_shape=jax.ShapeDtypeStruct((b, s, HEAD_DIM), F32),
        compiler_params=_cparams(("parallel",)),
        name="gdn_gates",
    )(p, nega_lane, dtb_lane)


def _each(f, *lists):
    return [f(*a) for a in zip(*lists)]


def _unit_tri_inverse_minus_eye(ms, blk8, level_masks):
    nm = _each(lambda m: jnp.where(blk8, -m, 0.0), ms)
    n2 = _each(lambda a: _dot(a, a), nm)
    n4 = _each(lambda a: _dot(a, a), n2)
    e = _each(lambda a, a2: a + a2 + _dot(a, a2), nm, n2)
    e = _each(lambda x, a4: x + a4 + _dot(x, a4), e, n4)
    for cm in level_masks:
        c = _each(lambda m: jnp.where(cm, m, 0.0), ms)
        tc = _each(lambda x, y: y + _dot(x, y), e, c)
        e = _each(lambda x, y: x - (y + _dot(y, x)), e, tc)
    return e


def _gdn_kernel(q_ref, k_ref, v_ref, wq_ref, wk_ref, wv_ref, gate_ref, z_ref, g_ref, y_ref,
                o_scr, ou_scr, od_scr, aq_scr, b_scr, gl_scr, xq_scr, xk_scr, xv_scr,
                *, n_chunks, n_ctx_chunks, n_heads, group):
    h = pl.program_id(1)
    L = CHUNK
    L2 = 2 * L
    s_rows = q_ref.shape[1]
    ii = lax.broadcasted_iota(jnp.int32, (L2, L2), 0)
    jj = lax.broadcasted_iota(jnp.int32, (L2, L2), 1)
    lane = lax.broadcasted_iota(jnp.int32, (L, HEAD_DIM), 1)
    up = ii < L
    down = jnp.logical_not(up)
    same64 = (ii >> 6) == (jj >> 6)
    same32 = (ii >> 5) == (jj >> 5)
    same16 = (ii >> 4) == (jj >> 4)
    same8 = (ii >> 3) == (jj >> 3)
    incl = same64 & ((up & (ii >= jj)) | (down & (ii <= jj)))
    strict = same64 & ((up & (ii > jj)) | (down & (ii < jj)))
    levels = (same16 & jnp.logical_not(same8), same32 & jnp.logical_not(same16),
              same64 & jnp.logical_not(same32))
    row8 = lax.broadcasted_iota(jnp.int32, (SUBLANES, HEAD_DIM), 0)
    zeros_wu = jnp.zeros((L, 2 * HEAD_DIM), BF16)
    taps = wq_ref.shape[0]
    halo = taps // 2

    def pick(gates, idx):
        return jnp.sum(jnp.where(lane == idx, gates, 0.0), axis=1, keepdims=True)

    n_ctx_rows = n_ctx_chunks * L
    blk_rows = group * L
    zpad = jnp.zeros((CONV_PAD, HEAD_DIM), F32)
    for src, dst in ((q_ref, xq_scr), (k_ref, xk_scr), (v_ref, xv_scr)):
        dst[pl.ds(0, CONV_PAD), :] = zpad
        dst[pl.ds(CONV_PAD + n_ctx_rows, CONV_PAD), :] = zpad
        dst[pl.ds(2 * CONV_PAD + s_rows, CONV_PAD), :] = zpad

        def fill(n, _, src=src, dst=dst):
            r0 = pl.multiple_of(n * blk_rows, blk_rows)
            off = jnp.where(r0 < n_ctx_rows, CONV_PAD, 2 * CONV_PAD)
            dst[pl.ds(pl.multiple_of(r0 + off, SUBLANES), blk_rows), :] = src[0, pl.ds(r0, blk_rows), :].astype(F32)
            return 0

        lax.fori_loop(0, n_chunks // group, fill, 0)

    def conv_silu(x_scr, w_ref, c):
        base = c * L + jnp.where(c < n_ctx_chunks, CONV_PAD, 2 * CONV_PAD) - halo
        w = w_ref[...]
        acc = x_scr[pl.ds(base, L), :] * w[0:1]
        for j in range(1, taps):
            acc = acc + x_scr[pl.ds(base + j, L), :] * w[j:j + 1]
        return _silu(acc)

    def l2(x):
        return x * lax.rsqrt(jnp.sum(x * x, axis=-1, keepdims=True) + EPS)

    def stack2(x):
        return jnp.concatenate([x, x], axis=0)

    def prepare_chunks(cs):
        q = _each(lambda c: stack2(l2(conv_silu(xq_scr, wq_ref, c)) * (HEAD_DIM ** -0.5)), cs)
        k2 = _each(lambda c: stack2(l2(conv_silu(xk_scr, wk_ref, c))), cs)
        v2 = _each(lambda c: stack2(conv_silu(xv_scr, wv_ref, c)), cs)
        gates = _each(lambda c: gate_ref[0, pl.ds(pl.multiple_of(c * L, L), L), :], cs)
        beta2 = _each(lambda g: jnp.concatenate([pick(g, h), pick(g, n_heads + h)], axis=0), gates)
        pf = _each(lambda g: pick(g, 2 * n_heads + h), gates)
        pb = _each(lambda g: pick(g, 3 * n_heads + h), gates)
        p2 = _each(lambda x, y: jnp.concatenate([x, y], axis=0), pf, pb)
        gtot2 = _each(lambda x, y: jnp.concatenate([jnp.broadcast_to(x[L - 1:L], (L, 1)),
                                                    jnp.broadcast_to(y[0:1], (L, 1))], axis=0), pf, pb)
        pcol = _each(lambda x: jnp.broadcast_to(x, (L2, L2)), p2)
        diff = _each(lambda x: x - x.T, pcol)
        decay = _each(lambda x: jnp.where(incl, jnp.exp(jnp.where(incl, x, 0.0)), 0.0), diff)
        a = _each(lambda kk, qq: _dot_nt(jnp.concatenate([kk, qq], axis=0), kk), k2, q)
        m = _each(lambda bt, x, dc: jnp.where(strict, bt * x[:L2] * dc, 0.0), beta2, a, decay)
        qk = _each(lambda x, dc: (x[L2:] * dc).astype(BF16), a, decay)
        e = _unit_tri_inverse_minus_eye(m, same8, levels)
        e_in = _each(jnp.exp, p2)
        rhs = _each(lambda kk, vv, bt, ei: jnp.concatenate([kk * (bt * ei), vv * bt], axis=1),
                    k2, v2, beta2, e_in)
        wu = _each(lambda x, r: (r + _dot(x, r)).astype(BF16), e, rhs)
        kdt = _each(lambda kk, gt, pp: (kk * jnp.exp(gt - pp)).T.astype(BF16), k2, gtot2, p2)
        ab_up = _each(lambda kt, x: jnp.dot(kt, jnp.concatenate([x[:L], zeros_wu], axis=0),
                                            preferred_element_type=F32), kdt, wu)
        ab_dn = _each(lambda kt, x: jnp.dot(kt, jnp.concatenate([zeros_wu, x[L:]], axis=0),
                                            preferred_element_type=F32), kdt, wu)
        qo = _each(lambda x, y: jnp.dot(x, y, preferred_element_type=F32), qk, wu)
        qp = _each(lambda qq, ei, x: qq * ei - x[:, :HEAD_DIM], q, e_in, qo)
        for g, c in enumerate(cs):
            o_scr[pl.ds(pl.multiple_of(c * L, L), L), :] = qo[g][:L, HEAD_DIM:] + qo[g][L:, HEAD_DIM:]
            aq_scr[0, c] = jnp.concatenate([ab_up[g][:, :HEAD_DIM], qp[g][:L]], axis=0).astype(BF16)
            aq_scr[1, c] = jnp.concatenate([ab_dn[g][:, :HEAD_DIM], qp[g][L:]], axis=0).astype(BF16)
            b_scr[0, c] = ab_up[g][:, HEAD_DIM:]
            b_scr[1, c] = ab_dn[g][:, HEAD_DIM:]
            gl_scr[c] = jnp.exp(jnp.where(row8 == 0,
                                          jnp.broadcast_to(pf[g][L - 1:L], (SUBLANES, HEAD_DIM)),
                                          jnp.broadcast_to(pb[g][0:1], (SUBLANES, HEAD_DIM))))

    def prepare_group(i, _):
        prepare_chunks([i * GDN_PREP_GROUP + g for g in range(GDN_PREP_GROUP)])
        return 0

    n_full = n_chunks // GDN_PREP_GROUP
    if n_full:
        lax.fori_loop(0, n_full, prepare_group, 0)
    if n_chunks % GDN_PREP_GROUP:
        prepare_chunks([jnp.int32(c) for c in range(n_full * GDN_PREP_GROUP, n_chunks)])

    def body(n, carry):
        s_up, s_dn = carry
        cb = _backward_chunk(n, n_chunks, n_ctx_chunks)
        r_up = jnp.dot(aq_scr[0, n], s_up.astype(BF16), preferred_element_type=F32)
        r_dn = jnp.dot(aq_scr[1, cb], s_dn.astype(BF16), preferred_element_type=F32)
        ou_scr[pl.ds(pl.multiple_of(n * L, L), L), :] = r_up[L2:]
        od_scr[pl.ds(pl.multiple_of(cb * L, L), L), :] = r_dn[L2:]
        s_up = s_up * gl_scr[n, 0:1, :] + (b_scr[0, n] - r_up[:L2])
        s_dn = s_dn * gl_scr[cb, 1:2, :] + (b_scr[1, cb] - r_dn[:L2])
        return s_up, s_dn

    zero = jnp.zeros((HEAD_DIM, HEAD_DIM), F32)
    lax.fori_loop(0, n_chunks, body, (zero, zero))

    rows = group * L

    def epilogue(n, _):
        r0 = pl.multiple_of(n * rows, rows)
        o = o_scr[pl.ds(r0, rows), :] + (ou_scr[pl.ds(r0, rows), :] + od_scr[pl.ds(r0, rows), :])
        y = _rms(o) * g_ref[...]
        z = z_ref[0, pl.ds(r0, rows), :].astype(F32)
        y_ref[0, pl.ds(r0, rows), :] = (y * _silu(z)).astype(y_ref.dtype)
        return 0

    lax.fori_loop(0, n_chunks // group, epilogue, 0)


def _gdn(p, conv_w, gates, onorm_g, n_ctx, n_heads, col_qkv, col_z):
    b, s, _ = p.shape
    nc = s // CHUNK
    taps = conv_w.shape[0]
    kern = functools.partial(_gdn_kernel, n_chunks=nc, n_ctx_chunks=n_ctx // CHUNK, n_heads=n_heads,
                             group=_group(nc))
    seq = lambda off: pl.BlockSpec((1, s, HEAD_DIM), lambda bb, h: (bb, 0, off + h))
    wsp = lambda off: pl.BlockSpec((taps, HEAD_DIM), lambda bb, h: (0, off + h))
    qb = col_qkv // HEAD_DIM
    return pl.pallas_call(
        kern,
        grid=(b, n_heads),
        in_specs=[seq(qb), seq(qb + n_heads), seq(qb + 2 * n_heads),
                  wsp(0), wsp(n_heads), wsp(2 * n_heads),
                  pl.BlockSpec((1, s, HEAD_DIM), lambda bb, h: (bb, 0, 0)),
                  seq(col_z // HEAD_DIM),
                  pl.BlockSpec((1, HEAD_DIM), lambda bb, h: (0, 0))],
        out_specs=seq(0),
        out_shape=jax.ShapeDtypeStruct((b, s, n_heads * HEAD_DIM), BF16),
        scratch_shapes=[pltpu.VMEM((s, HEAD_DIM), F32)] * 3
                       + [pltpu.VMEM((2, nc, 3 * CHUNK, HEAD_DIM), BF16),
                          pltpu.VMEM((2, nc, 2 * CHUNK, HEAD_DIM), F32),
                          pltpu.VMEM((nc, SUBLANES, HEAD_DIM), F32)]
                       + [pltpu.VMEM((s + 3 * CONV_PAD, HEAD_DIM), F32)] * 3,
        compiler_params=_cparams(("parallel", "arbitrary")),
        name="gated_deltanet",
    )(p, p, p, conv_w, conv_w, conv_w, gates, p, onorm_g.reshape(1, HEAD_DIM))


def _ret_kernel(lg_ref, q_ref, k_ref, v_ref, z_ref, c2_ref, s2_ref, g_ref, y_ref,
                o_scr, q_scr, kzt_scr, *, n_chunks, n_ctx_chunks, group):
    h = pl.program_id(1)
    L = CHUNK
    L2 = 2 * L
    ii = lax.broadcasted_iota(jnp.int32, (L, L), 0)
    jj = lax.broadcasted_iota(jnp.int32, (L, L), 1)
    row_i = lax.broadcasted_iota(jnp.int32, (L, HEAD_DIM), 0).astype(F32)
    row2 = lax.broadcasted_iota(jnp.int32, (L2, HEAD_DIM), 0).astype(F32)
    dist = (ii - jj).astype(F32)
    lgs = (lg_ref[0, h], lg_ref[1, h])
    dsum = (jnp.where(ii >= jj, jnp.exp(jnp.maximum(dist, 0.0) * lgs[0]), 0.0)
            + jnp.where(ii <= jj, jnp.exp(jnp.maximum(-dist, 0.0) * lgs[1]), 0.0))
    xi = (jnp.exp((row_i + 1.0) * lgs[0]), jnp.exp((L - row_i) * lgs[1]))
    zeta2 = jnp.where(row2 < L, jnp.exp((L - 1.0 - row2) * lgs[0]), jnp.exp((row2 - L) * lgs[1]))
    gch = tuple(jnp.exp(jnp.full((1, HEAD_DIM), float(L), F32) * lg) for lg in lgs)
    zeros = jnp.zeros((L, HEAD_DIM), v_ref.dtype)

    def prepare_group(i, _):
        cs = [i * group + g for g in range(group)]
        rows = [pl.ds(pl.multiple_of(c * L, L), L) for c in cs]
        q = _each(lambda r: _rope(q_ref[0, r, :].astype(F32), c2_ref[r, :], s2_ref[r, :]).astype(BF16), rows)
        k = _each(lambda r: _rope(k_ref[0, r, :].astype(F32), c2_ref[r, :], s2_ref[r, :])
                  * (HEAD_DIM ** -0.5), rows)
        sc = _each(lambda qq, kk: _dot_nt(qq, kk) * dsum, q, k)
        intra = _each(lambda x, r: _dot(x, v_ref[0, r, :]), sc, rows)
        kzt = _each(lambda kk: (jnp.concatenate([kk, kk], axis=0) * zeta2).T.astype(BF16), k)
        for g, c in enumerate(cs):
            q_scr[rows[g], :] = q[g]
            o_scr[rows[g], :] = intra[g]
            kzt_scr[c] = kzt[g]
        return 0

    lax.fori_loop(0, n_chunks // group, prepare_group, 0)

    def advance(c, d, state):
        r0 = pl.multiple_of(c * L, L)
        v = v_ref[0, pl.ds(r0, L), :]
        vpad = jnp.concatenate([v, zeros] if d == 0 else [zeros, v], axis=0)
        o_scr[pl.ds(r0, L), :] += xi[d] * jnp.dot(q_scr[pl.ds(r0, L), :], state.astype(BF16),
                                                  preferred_element_type=F32)
        return state * gch[d] + jnp.dot(kzt_scr[c], vpad, preferred_element_type=F32)

    def body(n, carry):
        s_up, s_dn = carry
        s_up = advance(n, 0, s_up)
        s_dn = advance(_backward_chunk(n, n_chunks, n_ctx_chunks), 1, s_dn)
        return s_up, s_dn

    zero = jnp.zeros((HEAD_DIM, HEAD_DIM), F32)
    lax.fori_loop(0, n_chunks, body, (zero, zero), unroll=2)

    rows = group * L

    def epilogue(n, _):
        r0 = pl.multiple_of(n * rows, rows)
        o = o_scr[pl.ds(r0, rows), :]
        mu = jnp.mean(o, axis=-1, keepdims=True)
        var = jnp.mean(jnp.square(o - mu), axis=-1, keepdims=True)
        y = (o - mu) * lax.rsqrt(var + EPS) * g_ref[...]
        z = z_ref[0, pl.ds(r0, rows), :].astype(F32)
        y_ref[0, pl.ds(r0, rows), :] = (y * _silu(z)).astype(y_ref.dtype)
        return 0

    lax.fori_loop(0, n_chunks // group, epilogue, 0)


def _retention(lg, p, c2, s2, gnorm_g, n_ctx, n_heads, col_q, col_k, col_v, col_z):
    b, s, _ = p.shape
    nc = s // CHUNK
    kern = functools.partial(_ret_kernel, n_chunks=nc, n_ctx_chunks=n_ctx // CHUNK, group=_group(nc))
    seq = lambda off: pl.BlockSpec((1, s, HEAD_DIM), lambda bb, h: (bb, 0, off // HEAD_DIM + h))
    tab = pl.BlockSpec((s, HEAD_DIM), lambda bb, h: (0, 0))
    return pl.pallas_call(
        kern,
        grid=(b, n_heads),
        in_specs=[pl.BlockSpec(memory_space=pltpu.SMEM), seq(col_q), seq(col_k), seq(col_v), seq(col_z),
                  tab, tab, pl.BlockSpec((1, HEAD_DIM), lambda bb, h: (0, h))],
        out_specs=seq(0),
        out_shape=jax.ShapeDtypeStruct((b, s, n_heads * HEAD_DIM), BF16),
        scratch_shapes=[pltpu.VMEM((s, HEAD_DIM), F32),
                        pltpu.VMEM((s, HEAD_DIM), BF16),
                        pltpu.VMEM((nc, HEAD_DIM, 2 * CHUNK), BF16)],
        compiler_params=_cparams(("parallel", "arbitrary")),
        name="retention",
    )(lg, p, p, p, p, c2, s2, gnorm_g.reshape(1, n_heads * HEAD_DIM))


def _merge_kernel(ya_ref, yd_ref, yr_ref, wa_ref, wd_ref, wr_ref, ga_ref, gd_ref, gr_ref, o_ref):
    gate = lambda g_ref: _sigmoid(g_ref[0].astype(F32))
    acc = gate(ga_ref) * jnp.dot(ya_ref[0], wa_ref[0], preferred_element_type=F32)
    acc = acc + gate(gd_ref) * jnp.dot(yd_ref[0], wd_ref[0], preferred_element_type=F32)
    acc = acc + gate(gr_ref) * jnp.dot(yr_ref[0], wr_ref[0], preferred_element_type=F32)
    o_ref[0] = acc.astype(o_ref.dtype)


def _merge(ya, yd, yr, w_branch, layer, p, col_gates, tm, tn=COL_TILE):
    b, s, bw = ya.shape
    d = w_branch.shape[2]
    gb = col_gates // tn
    ysp = pl.BlockSpec((1, tm, bw), lambda bb, i, j: (bb, i, 0))
    wsp = lambda br: pl.BlockSpec((1, bw, tn), lambda bb, i, j: (layer * N_BRANCH + br, 0, j))
    gsp = lambda br: pl.BlockSpec((1, tm, tn), lambda bb, i, j: (bb, i, gb + br * (d // tn) + j))
    return pl.pallas_call(
        _merge_kernel,
        grid=(b, s // tm, d // tn),
        in_specs=[ysp, ysp, ysp, wsp(0), wsp(1), wsp(2), gsp(0), gsp(1), gsp(2)],
        out_specs=pl.BlockSpec((1, tm, tn), lambda bb, i, j: (bb, i, j)),
        out_shape=jax.ShapeDtypeStruct((b, s, d), BF16),
        compiler_params=_cparams(("parallel", "parallel", "arbitrary")),
        name="branch_merge",
    )(ya, yd, yr, w_branch, w_branch, w_branch, p, p, p)


def _outproj_kernel(m_ref, w_ref, x_ref, lat_ref, ctx_ref, o_ref, *, n_ctx, tm):
    i = pl.program_id(1)
    row = i * tm + lax.broadcasted_iota(jnp.int32, (tm, 1), 0)
    gate = jnp.where(row < n_ctx, ctx_ref[...], lat_ref[0])
    o_ref[0] = x_ref[0] + gate * jnp.dot(m_ref[0], w_ref[0], preferred_element_type=F32)


def _out_projection(merged, w_out, layer, xs, gate_lat, gate_ctx, n_ctx, tm, tn=COL_TILE):
    b, s, d = xs.shape
    kern = functools.partial(_outproj_kernel, n_ctx=n_ctx, tm=tm)
    return pl.pallas_call(
        kern,
        grid=(b, s // tm, d // tn),
        in_specs=[pl.BlockSpec((1, tm, d), lambda bb, i, j: (bb, i, 0)),
                  pl.BlockSpec((1, d, tn), lambda bb, i, j: (layer, 0, j)),
                  pl.BlockSpec((1, tm, tn), lambda bb, i, j: (bb, i, j)),
                  pl.BlockSpec((1, 1, tn), lambda bb, i, j: (bb, 0, j)),
                  pl.BlockSpec((1, tn), lambda bb, i, j: (0, j))],
        out_specs=pl.BlockSpec((1, tm, tn), lambda bb, i, j: (bb, i, j)),
        out_shape=jax.ShapeDtypeStruct((b, s, d), F32),
        compiler_params=_cparams(("parallel", "parallel", "arbitrary")),
        name="out_projection",
    )(merged, w_out, xs, gate_lat, gate_ctx)


def _cast_kernel(x_ref, o_ref):
    o_ref[...] = x_ref[...].astype(o_ref.dtype)


def _to_bf16(x, rows=512):
    r, c = x.shape
    spec = pl.BlockSpec((rows, c), lambda i: (i, 0))
    return pl.pallas_call(
        _cast_kernel,
        grid=(r // rows,),
        in_specs=[spec],
        out_specs=spec,
        out_shape=jax.ShapeDtypeStruct((r, c), BF16),
        compiler_params=_cparams(("parallel",)),
        name="weight_cast",
    )(x)


def _deinterleave(w, n_heads):
    lead = w.shape[:-1]
    w = w.reshape(*lead, n_heads, HALF, 2)
    return jnp.swapaxes(w, -1, -2).reshape(*lead, n_heads * HEAD_DIM)


class _Cols:
    def __init__(self, d_model):
        aw = d_model // 2
        kvw = aw // ATT_GROUPS
        nh = aw // HEAD_DIM
        self.src = dict(att_q=aw, att_k=kvw, att_v=kvw, att_z=aw, dn_qkv=3 * aw, dn_z=aw,
                        dn_b=2 * nh, dn_a=2 * nh, ret_q=aw, ret_k=aw, ret_v=aw, ret_z=aw,
                        gates=N_BRANCH * d_model)
        self.order = ("att_q", "att_k", "att_v", "att_z", "dn_qkv", "dn_z", "ret_q", "ret_k", "ret_v",
                      "ret_z", "dn_b", "dn_a", "gates")
        self.off = {}
        o = 0
        for name in self.order:
            if name == "gates":
                o = -(-o // COL_TILE) * COL_TILE
            self.off[name] = o
            o += self.src[name]
        self.width = -(-o // COL_TILE) * COL_TILE
        self.n_heads = nh


def _prep_w_in(w_in, cols):
    offs, o = {}, 0
    for name, width in cols.src.items():
        offs[name] = o
        o += width
    parts, filled = [], 0
    for name in cols.order:
        if cols.off[name] > filled:
            parts.append(jnp.zeros((w_in.shape[0], cols.off[name] - filled), w_in.dtype))
        seg = w_in[:, offs[name]:offs[name] + cols.src[name]]
        if name in ("att_q", "att_k", "ret_q", "ret_k"):
            seg = _deinterleave(seg, cols.src[name] // HEAD_DIM)
        parts.append(seg)
        filled = cols.off[name] + cols.src[name]
    if cols.width > filled:
        parts.append(jnp.zeros((w_in.shape[0], cols.width - filled), w_in.dtype))
    return jnp.concatenate(parts, axis=1).astype(BF16)


def _rope_tables(n_ctx, t):
    rows = t // GRID_W
    row, col = jnp.meshgrid(jnp.arange(rows, dtype=F32), jnp.arange(GRID_W, dtype=F32), indexing="ij")
    n_freq = HEAD_DIM // 4
    inv = ROPE_THETA ** (-jnp.arange(n_freq, dtype=F32) / n_freq)
    ang = jnp.concatenate([row.reshape(-1, 1) * inv, col.reshape(-1, 1) * inv], axis=-1)
    cos, sin = jnp.cos(ang), jnp.sin(ang)
    cos = jnp.concatenate([jnp.ones((n_ctx, HALF), F32), cos], axis=0)
    sin = jnp.concatenate([jnp.zeros((n_ctx, HALF), F32), sin], axis=0)
    return jnp.concatenate([cos, cos], axis=-1), jnp.concatenate([-sin, sin], axis=-1)


def _row_tile(s):
    for tm in (1088, 1024, 544, 512, 256, 128):
        if s % tm == 0:
            return tm
    return s


def kernel(x, c, ctx, c_ctx, norm_g, w_mod, b_mod, w_in, att_qnorm_g, att_knorm_g, dn_conv_w,
           dn_a_log, dn_dt_bias, dn_onorm_g, ret_gnorm_g, w_branch, w_out):
    b, t, d = x.shape
    n_ctx = ctx.shape[1]
    depth = w_in.shape[0]
    s = n_ctx + t
    cols = _Cols(d)
    nh = cols.n_heads
    tm = _row_tile(s)

    xs = jnp.concatenate([ctx, x], axis=1)
    c2, s2 = _rope_tables(n_ctx, t)

    rows = -(-(b + 1) // SUBLANES) * SUBLANES
    cc = jnp.zeros((rows, d), F32).at[:b].set(c).at[b].set(c_ctx)
    mod = _modulation(cc, w_mod, b_mod)

    decay = jnp.exp(jnp.linspace(math.log(1.0 / 32), math.log(1.0 / 512), nh, dtype=F32))
    log_gamma = jnp.log1p(-decay)
    lg = jnp.stack([log_gamma, log_gamma[::-1]])

    bw = w_branch.shape[2]
    wb_all = _to_bf16(w_branch.reshape(depth * N_BRANCH * bw, d)).reshape(depth * N_BRANCH, bw, d)
    wo_all = _to_bf16(w_out.reshape(depth * d, d)).reshape(depth, d, d)

    pad_l = jnp.zeros((2 * nh,), F32)
    pad_r = jnp.zeros((HEAD_DIM - 4 * nh,), F32)
    off = cols.off
    for l in range(depth):
        m3 = mod[l].reshape(rows, 3, d)
        mod_lat, mod_ctx = m3[:b], m3[b]
        w = _prep_w_in(w_in[l], cols)
        p = _in_projection(xs, norm_g[l], mod_lat, mod_ctx, w, n_ctx, tm)

        ya = _attention(p, c2, s2, _deinterleave(att_qnorm_g[l], 1), _deinterleave(att_knorm_g[l], 1),
                        n_ctx, nh, off["att_q"], off["att_k"], off["att_v"], off["att_z"])

        nega_lane = jnp.concatenate([pad_l, -jnp.exp(dn_a_log[l].astype(F32)).reshape(-1), pad_r])
        dtb_lane = jnp.concatenate([pad_l, dn_dt_bias[l].astype(F32).reshape(-1), pad_r])
        gates = _gdn_gates(p, nega_lane.reshape(1, HEAD_DIM), dtb_lane.reshape(1, HEAD_DIM), off["dn_b"], nh)
        yd = _gdn(p, dn_conv_w[l].astype(F32), gates, dn_onorm_g[l], n_ctx, nh, off["dn_qkv"], off["dn_z"])

        yr = _retention(lg, p, c2, s2, ret_gnorm_g[l], n_ctx, nh,
                        off["ret_q"], off["ret_k"], off["ret_v"], off["ret_z"])

        merged = _merge(ya, yd, yr, wb_all, l, p, off["gates"], tm)
        xs = _out_projection(merged, wo_all, l, xs, mod_lat[:, 2:3, :], mod_ctx[2:3, :], n_ctx, tm)

    return xs[:, n_ctx:, :]
```

```python
import functools
import math

import jax
import jax.numpy as jnp
from jax import lax
from jax.experimental import pallas as pl
from jax.experimental.pallas import tpu as pltpu

F32 = jnp.float32
BF16 = jnp.bfloat16

HEAD_DIM = 128
HALF = HEAD_DIM // 2
GRID_W = 64
ATT_GROUPS = 4
ATT_STACK = 2
CHUNK = 64
SUBLANES = 8
CONV_PAD = 16
COL_TILE = 1024
GDN_PREP_GROUP = 12
NORM_SPLIT = 4
ROPE_THETA = 10000.0
EPS = 1e-6
N_BRANCH = 3
V7X_VMEM_LIMIT = 56 * 1024 * 1024


def _sigmoid(x):
    return 1.0 / (1.0 + jnp.exp(-x))


def _silu(x):
    return x * _sigmoid(x)


def _softplus(x):
    return jnp.maximum(x, 0.0) + jnp.log1p(jnp.exp(-jnp.abs(x)))


def _dot(a, b):
    return jnp.dot(a.astype(BF16), b.astype(BF16), preferred_element_type=F32)


def _dot_nt(a, b):
    return lax.dot_general(a.astype(BF16), b.astype(BF16), (((1,), (1,)), ((), ())),
                           preferred_element_type=F32)


def _split3(x):
    hi = x.astype(BF16)
    r = x - hi.astype(F32)
    mid = r.astype(BF16)
    lo = (r - mid.astype(F32)).astype(BF16)
    return hi, mid, lo


def _dot_sel_l(sel, x):
    hi, mid, lo = _split3(x)
    d = lambda v: jnp.dot(sel, v, preferred_element_type=F32)
    return d(hi) + (d(mid) + d(lo))


def _rope(x, c2, s2):
    return x * c2 + pltpu.roll(x, HALF, 1) * s2


def _rms(x):
    return x * lax.rsqrt(jnp.mean(x * x, axis=-1, keepdims=True) + EPS)


def _cparams(sem):
    return pltpu.CompilerParams(dimension_semantics=sem, vmem_limit_bytes=V7X_VMEM_LIMIT)


def _group(n_chunks):
    for g in (4, 2):
        if n_chunks % g == 0:
            return g
    return 1


def _mod_kernel(c_ref, w_ref, b_ref, o_ref):
    a = _silu(c_ref[...])
    o_ref[0] = _dot(a, w_ref[0]) + b_ref[0]


def _modulation(cc, w_mod, b_mod, tn=512):
    depth, d, n = w_mod.shape
    rows = cc.shape[0]
    return pl.pallas_call(
        _mod_kernel,
        grid=(depth, n // tn),
        in_specs=[pl.BlockSpec((rows, d), lambda l, j: (0, 0)),
                  pl.BlockSpec((1, d, tn), lambda l, j: (l, 0, j)),
                  pl.BlockSpec((1, 1, tn), lambda l, j: (l, 0, j))],
        out_specs=pl.BlockSpec((1, rows, tn), lambda l, j: (l, 0, j)),
        out_shape=jax.ShapeDtypeStruct((depth, rows, n), F32),
        compiler_params=_cparams(("parallel", "parallel")),
        name="modulation",
    )(cc, w_mod, b_mod.reshape(depth, 1, n))


def _inproj_kernel(x_ref, g_ref, lat_ref, ctx_ref, w_ref, o_ref, h_scr, *, n_ctx, tm, d):
    i = pl.program_id(1)
    j = pl.program_id(2)

    @pl.when(j == 0)
    def _():
        rb = tm // NORM_SPLIT

        def norm_rows(n, _):
            r0 = pl.multiple_of(n * rb, rb)
            y = _rms(x_ref[0, pl.ds(r0, rb), :]) * g_ref[...]
            row = i * tm + r0 + lax.broadcasted_iota(jnp.int32, (rb, 1), 0)
            is_ctx = row < n_ctx
            shift = jnp.where(is_ctx, ctx_ref[0:1, :], lat_ref[0, 0:1, :])
            scale = jnp.where(is_ctx, ctx_ref[1:2, :], lat_ref[0, 1:2, :])
            h_scr[pl.ds(r0, rb), :] = (y * (1.0 + scale) + shift).astype(BF16)
            return 0

        lax.fori_loop(0, NORM_SPLIT, norm_rows, 0)

    o_ref[0] = jnp.dot(h_scr[...], w_ref[...], preferred_element_type=F32).astype(o_ref.dtype)


def _in_projection(xs, norm_g, mod_lat, mod_ctx, w, n_ctx, tm, tn=COL_TILE):
    b, s, d = xs.shape
    n = w.shape[1]
    kern = functools.partial(_inproj_kernel, n_ctx=n_ctx, tm=tm, d=d)
    return pl.pallas_call(
        kern,
        grid=(b, s // tm, n // tn),
        in_specs=[pl.BlockSpec((1, tm, d), lambda bb, i, j: (bb, i, 0)),
                  pl.BlockSpec((1, d), lambda bb, i, j: (0, 0)),
                  pl.BlockSpec((1, 3, d), lambda bb, i, j: (bb, 0, 0)),
                  pl.BlockSpec((3, d), lambda bb, i, j: (0, 0)),
                  pl.BlockSpec((d, tn), lambda bb, i, j: (0, j))],
        out_specs=pl.BlockSpec((1, tm, tn), lambda bb, i, j: (bb, i, j)),
        out_shape=jax.ShapeDtypeStruct((b, s, n), BF16),
        scratch_shapes=[pltpu.VMEM((tm, d), BF16)],
        compiler_params=_cparams(("parallel", "parallel", "arbitrary")),
        name="in_projection",
    )(xs, norm_g.reshape(1, d), mod_lat, mod_ctx, w)


def _attn_kernel(q_ref, k_ref, v_ref, z_ref, c2q_ref, s2q_ref, c2_ref, s2_ref, qn_ref, kn_ref,
                 y_ref, k_scr, v_scr, *, n_ctx, tq, blk):
    i = pl.program_id(2)
    s = k_ref.shape[1]

    @pl.when(i == 0)
    def _():
        def prep(n, _):
            r0 = pl.multiple_of(n * blk, blk)
            k = _rms(k_ref[0, pl.ds(r0, blk), :].astype(F32)) * kn_ref[...]
            k = _rope(k, c2_ref[pl.ds(r0, blk), :], s2_ref[pl.ds(r0, blk), :])
            k_scr[pl.ds(r0, blk), :] = k.astype(BF16)
            v_scr[pl.ds(r0, blk), :HEAD_DIM] = v_ref[0, pl.ds(r0, blk), :]
            v_scr[pl.ds(r0, blk), HEAD_DIM:] = jnp.ones((blk, HEAD_DIM), BF16)
            return 0

        lax.fori_loop(0, s // blk, prep, 0)

    def query(g):
        q = q_ref[0, :, g * HEAD_DIM:(g + 1) * HEAD_DIM].astype(F32)
        q = _rope(_rms(q) * qn_ref[...], c2q_ref[...], s2q_ref[...])
        return (q * (HEAD_DIM ** -0.5)).astype(BF16)

    qs = _each(query, list(range(ATT_GROUPS)))
    stacks = [jnp.concatenate(qs[g:g + ATT_STACK], axis=0) for g in range(0, ATT_GROUPS, ATT_STACK)]

    def attend(kv_len):
        k = k_scr[:kv_len, :]
        v1 = v_scr[:kv_len, :]
        sc = _each(lambda q: lax.dot_general(q, k, (((1,), (1,)), ((), ())),
                                             preferred_element_type=F32), stacks)
        m = _each(lambda x: jnp.max(x, axis=-1, keepdims=True), sc)
        p = _each(lambda x, mx: jnp.exp(x - mx).astype(BF16), sc, m)
        ol = _each(lambda x: jnp.dot(x, v1, preferred_element_type=F32), p)
        o = _each(lambda x: x[:, :HEAD_DIM] / x[:, HEAD_DIM:HEAD_DIM + 1], ol)
        heads = [x[g * tq:(g + 1) * tq] for x in o for g in range(ATT_STACK)]
        y = jnp.concatenate(heads, axis=1) * _silu(z_ref[0].astype(F32))
        y_ref[0] = y.astype(y_ref.dtype)

    @pl.when(i < n_ctx // tq)
    def _():
        attend(n_ctx)

    @pl.when(i >= n_ctx // tq)
    def _():
        attend(s)


def _attention(p, c2, s2, qn, kn, n_ctx, n_heads, col_q, col_k, col_v, col_z, tq=256):
    b, s, _ = p.shape
    gw = ATT_GROUPS * HEAD_DIM
    kern = functools.partial(_attn_kernel, n_ctx=n_ctx, tq=tq, blk=n_ctx)
    row_blk = lambda col: pl.BlockSpec((1, tq, gw), lambda bb, h, i: (bb, i, col // gw + h))
    kv_blk = lambda col: pl.BlockSpec((1, s, HEAD_DIM), lambda bb, h, i: (bb, 0, col // HEAD_DIM + h))
    tab_q = pl.BlockSpec((tq, HEAD_DIM), lambda bb, h, i: (i, 0))
    tab = pl.BlockSpec((s, HEAD_DIM), lambda bb, h, i: (0, 0))
    vec = pl.BlockSpec((1, HEAD_DIM), lambda bb, h, i: (0, 0))
    return pl.pallas_call(
        kern,
        grid=(b, n_heads // ATT_GROUPS, s // tq),
        in_specs=[row_blk(col_q), kv_blk(col_k), kv_blk(col_v), row_blk(col_z),
                  tab_q, tab_q, tab, tab, vec, vec],
        out_specs=row_blk(0),
        out_shape=jax.ShapeDtypeStruct((b, s, n_heads * HEAD_DIM), BF16),
        scratch_shapes=[pltpu.VMEM((s, HEAD_DIM), BF16), pltpu.VMEM((s, 2 * HEAD_DIM), BF16)],
        compiler_params=_cparams(("parallel", "parallel", "arbitrary")),
        name="attention",
    )(p, p, p, p, c2, s2, c2, s2, qn.reshape(1, HEAD_DIM), kn.reshape(1, HEAD_DIM))


def _backward_chunk(n, n_chunks, n_ctx_chunks):
    return jnp.where(n < n_ctx_chunks, n_ctx_chunks - 1 - n, n_chunks + n_ctx_chunks - 1 - n)


def _gdn_gates_kernel(raw_ref, nega_ref, dtb_ref, o_ref, *, n_chunks, n_heads):
    L = CHUNK
    ii = lax.broadcasted_iota(jnp.int32, (L, L), 0)
    jj = lax.broadcasted_iota(jnp.int32, (L, L), 1)
    lower_b = jnp.where(ii >= jj, 1.0, 0.0).astype(BF16)
    upper_b = jnp.where(ii <= jj, 1.0, 0.0).astype(BF16)
    lane = lax.broadcasted_iota(jnp.int32, (L, HEAD_DIM), 1)

    def body(c, _):
        r0 = pl.multiple_of(c * L, L)
        raw = raw_ref[0, pl.ds(r0, L), :].astype(F32)
        g = nega_ref[...] * _softplus(raw + dtb_ref[...])
        prefix = _dot_sel_l(lower_b, g)
        suffix = _dot_sel_l(upper_b, g)
        o_ref[0, pl.ds(r0, L), :] = jnp.where(lane < 2 * n_heads, _sigmoid(raw),
                                              jnp.where(lane < 3 * n_heads, prefix, suffix))
        return 0

    lax.fori_loop(0, n_chunks, body, 0)


def _gdn_gates(p, nega_lane, dtb_lane, col_ba, n_heads):
    b, s, _ = p.shape
    kern = functools.partial(_gdn_gates_kernel, n_chunks=s // CHUNK, n_heads=n_heads)
    return pl.pallas_call(
        kern,
        grid=(b,),
        in_specs=[pl.BlockSpec((1, s, HEAD_DIM), lambda bb: (bb, 0, col_ba // HEAD_DIM)),
                  pl.BlockSpec((1, HEAD_DIM), lambda bb: (0, 0)),
                  pl.BlockSpec((1, HEAD_DIM), lambda bb: (0, 0))],
        out_specs=pl.BlockSpec((1, s, HEAD_DIM), lambda bb: (bb, 0, 0)),
        out_shape=jax.ShapeDtypeStruct((b, s, HEAD_DIM), F32),
        compiler_params=_cparams(("parallel",)),
        name="gdn_gates",
    )(p, nega_lane, dtb_lane)


def _each(f, *lists):
    return [f(*a) for a in zip(*lists)]


def _unit_tri_inverse_minus_eye(ms, blk8, level_masks):
    nm = _each(lambda m: jnp.where(blk8, -m, 0.0), ms)
    n2 = _each(lambda a: _dot(a, a), nm)
    n4 = _each(lambda a: _dot(a, a), n2)
    e = _each(lambda a, a2: a + a2 + _dot(a, a2), nm, n2)
    e = _each(lambda x, a4: x + a4 + _dot(x, a4), e, n4)
    for cm in level_masks:
        c = _each(lambda m: jnp.where(cm, m, 0.0), ms)
        tc = _each(lambda x, y: y + _dot(x, y), e, c)
        e = _each(lambda x, y: x - (y + _dot(y, x)), e, tc)
    return e


def _gdn_kernel(q_ref, k_ref, v_ref, wq_ref, wk_ref, wv_ref, gate_ref, z_ref, g_ref, y_ref,
                o_scr, ou_scr, od_scr, aq_scr, b_scr, gl_scr, xq_scr, xk_scr, xv_scr,
                *, n_chunks, n_ctx_chunks, n_heads, group):
    h = pl.program_id(1)
    L = CHUNK
    L2 = 2 * L
    s_rows = q_ref.shape[1]
    ii = lax.broadcasted_iota(jnp.int32, (L2, L2), 0)
    jj = lax.broadcasted_iota(jnp.int32, (L2, L2), 1)
    lane = lax.broadcasted_iota(jnp.int32, (L, HEAD_DIM), 1)
    up = ii < L
    down = jnp.logical_not(up)
    same64 = (ii >> 6) == (jj >> 6)
    same32 = (ii >> 5) == (jj >> 5)
    same16 = (ii >> 4) == (jj >> 4)
    same8 = (ii >> 3) == (jj >> 3)
    incl = same64 & ((up & (ii >= jj)) | (down & (ii <= jj)))
    strict = same64 & ((up & (ii > jj)) | (down & (ii < jj)))
    levels = (same16 & jnp.logical_not(same8), same32 & jnp.logical_not(same16),
              same64 & jnp.logical_not(same32))
    row8 = lax.broadcasted_iota(jnp.int32, (SUBLANES, HEAD_DIM), 0)
    zeros_wu = jnp.zeros((L, 2 * HEAD_DIM), BF16)
    taps = wq_ref.shape[0]
    halo = taps // 2

    def pick(gates, idx):
        return jnp.sum(jnp.where(lane == idx, gates, 0.0), axis=1, keepdims=True)

    n_ctx_rows = n_ctx_chunks * L
    blk_rows = group * L
    zpad = jnp.zeros((CONV_PAD, HEAD_DIM), F32)
    for src, dst in ((q_ref, xq_scr), (k_ref, xk_scr), (v_ref, xv_scr)):
        dst[pl.ds(0, CONV_PAD), :] = zpad
        dst[pl.ds(CONV_PAD + n_ctx_rows, CONV_PAD), :] = zpad
        dst[pl.ds(2 * CONV_PAD + s_rows, CONV_PAD), :] = zpad

        def fill(n, _, src=src, dst=dst):
            r0 = pl.multiple_of(n * blk_rows, blk_rows)
            off = jnp.where(r0 < n_ctx_rows, CONV_PAD, 2 * CONV_PAD)
            dst[pl.ds(pl.multiple_of(r0 + off, SUBLANES), blk_rows), :] = src[0, pl.ds(r0, blk_rows), :].astype(F32)
            return 0

        lax.fori_loop(0, n_chunks // group, fill, 0)

    def conv_silu(x_scr, w_ref, c):
        base = c * L + jnp.where(c < n_ctx_chunks, CONV_PAD, 2 * CONV_PAD) - halo
        w = w_ref[...]
        acc = x_scr[pl.ds(base, L), :] * w[0:1]
        for j in range(1, taps):
            acc = acc + x_scr[pl.ds(base + j, L), :] * w[j:j + 1]
        return _silu(acc)

    def l2(x):
        return x * lax.rsqrt(jnp.sum(x * x, axis=-1, keepdims=True) + EPS)

    def stack2(x):
        return jnp.concatenate([x, x], axis=0)

    def prepare_chunks(cs):
        q = _each(lambda c: stack2(l2(conv_silu(xq_scr, wq_ref, c)) * (HEAD_DIM ** -0.5)), cs)
        k2 = _each(lambda c: stack2(l2(conv_silu(xk_scr, wk_ref, c))), cs)
        v2 = _each(lambda c: stack2(conv_silu(xv_scr, wv_ref, c)), cs)
        gates = _each(lambda c: gate_ref[0, pl.ds(pl.multiple_of(c * L, L), L), :], cs)
        beta2 = _each(lambda g: jnp.concatenate([pick(g, h), pick(g, n_heads + h)], axis=0), gates)
        pf = _each(lambda g: pick(g, 2 * n_heads + h), gates)
        pb = _each(lambda g: pick(g, 3 * n_heads + h), gates)
        p2 = _each(lambda x, y: jnp.concatenate([x, y], axis=0), pf, pb)
        gtot2 = _each(lambda x, y: jnp.concatenate([jnp.broadcast_to(x[L - 1:L], (L, 1)),
                                                    jnp.broadcast_to(y[0:1], (L, 1))], axis=0), pf, pb)
        pcol = _each(lambda x: jnp.broadcast_to(x, (L2, L2)), p2)
        diff = _each(lambda x: x - x.T, pcol)
        decay = _each(lambda x: jnp.where(incl, jnp.exp(jnp.where(incl, x, 0.0)), 0.0), diff)
        a = _each(lambda kk, qq: _dot_nt(jnp.concatenate([kk, qq], axis=0), kk), k2, q)
        m = _each(lambda bt, x, dc: jnp.where(strict, bt * x[:L2] * dc, 0.0), beta2, a, decay)
        qk = _each(lambda x, dc: (x[L2:] * dc).astype(BF16), a, decay)
        e = _unit_tri_inverse_minus_eye(m, same8, levels)
        e_in = _each(jnp.exp, p2)
        rhs = _each(lambda kk, vv, bt, ei: jnp.concatenate([kk * (bt * ei), vv * bt], axis=1),
                    k2, v2, beta2, e_in)
        wu = _each(lambda x, r: (r + _dot(x, r)).astype(BF16), e, rhs)
        kdt = _each(lambda kk, gt, pp: (kk * jnp.exp(gt - pp)).T.astype(BF16), k2, gtot2, p2)
        ab_up = _each(lambda kt, x: jnp.dot(kt, jnp.concatenate([x[:L], zeros_wu], axis=0),
                                            preferred_element_type=F32), kdt, wu)
        ab_dn = _each(lambda kt, x: jnp.dot(kt, jnp.concatenate([zeros_wu, x[L:]], axis=0),
                                            preferred_element_type=F32), kdt, wu)
        qo = _each(lambda x, y: jnp.dot(x, y, preferred_element_type=F32), qk, wu)
        qp = _each(lambda qq, ei, x: qq * ei - x[:, :HEAD_DIM], q, e_in, qo)
        for g, c in enumerate(cs):
            o_scr[pl.ds(pl.multiple_of(c * L, L), L), :] = qo[g][:L, HEAD_DIM:] + qo[g][L:, HEAD_DIM:]
            aq_scr[0, c] = jnp.concatenate([ab_up[g][:, :HEAD_DIM], qp[g][:L]], axis=0).astype(BF16)
            aq_scr[1, c] = jnp.concatenate([ab_dn[g][:, :HEAD_DIM], qp[g][L:]], axis=0).astype(BF16)
            b_scr[0, c] = ab_up[g][:, HEAD_DIM:]
            b_scr[1, c] = ab_dn[g][:, HEAD_DIM:]
            gl_scr[c] = jnp.exp(jnp.where(row8 == 0,
                                          jnp.broadcast_to(pf[g][L - 1:L], (SUBLANES, HEAD_DIM)),
                                          jnp.broadcast_to(pb[g][0:1], (SUBLANES, HEAD_DIM))))

    def prepare_group(i, _):
        prepare_chunks([i * GDN_PREP_GROUP + g for g in range(GDN_PREP_GROUP)])
        return 0

    n_full = n_chunks // GDN_PREP_GROUP
    if n_full:
        lax.fori_loop(0, n_full, prepare_group, 0)
    if n_chunks % GDN_PREP_GROUP:
        prepare_chunks([jnp.int32(c) for c in range(n_full * GDN_PREP_GROUP, n_chunks)])

    def body(n, carry):
        s_up, s_dn = carry
        cb = _backward_chunk(n, n_chunks, n_ctx_chunks)
        r_up = jnp.dot(aq_scr[0, n], s_up.astype(BF16), preferred_element_type=F32)
        r_dn = jnp.dot(aq_scr[1, cb], s_dn.astype(BF16), preferred_element_type=F32)
        ou_scr[pl.ds(pl.multiple_of(n * L, L), L), :] = r_up[L2:]
        od_scr[pl.ds(pl.multiple_of(cb * L, L), L), :] = r_dn[L2:]
        s_up = s_up * gl_scr[n, 0:1, :] + (b_scr[0, n] - r_up[:L2])
        s_dn = s_dn * gl_scr[cb, 1:2, :] + (b_scr[1, cb] - r_dn[:L2])
        return s_up, s_dn

    zero = jnp.zeros((HEAD_DIM, HEAD_DIM), F32)
    lax.fori_loop(0, n_chunks, body, (zero, zero))

    rows = group * L

    def epilogue(n, _):
        r0 = pl.multiple_of(n * rows, rows)
        o = o_scr[pl.ds(r0, rows), :] + (ou_scr[pl.ds(r0, rows), :] + od_scr[pl.ds(r0, rows), :])
        y = _rms(o) * g_ref[...]
        z = z_ref[0, pl.ds(r0, rows), :].astype(F32)
        y_ref[0, pl.ds(r0, rows), :] = (y * _silu(z)).astype(y_ref.dtype)
        return 0

    lax.fori_loop(0, n_chunks // group, epilogue, 0)


def _gdn(p, conv_w, gates, onorm_g, n_ctx, n_heads, col_qkv, col_z):
    b, s, _ = p.shape
    nc = s // CHUNK
    taps = conv_w.shape[0]
    kern = functools.partial(_gdn_kernel, n_chunks=nc, n_ctx_chunks=n_ctx // CHUNK, n_heads=n_heads,
                             group=_group(nc))
    seq = lambda off: pl.BlockSpec((1, s, HEAD_DIM), lambda bb, h: (bb, 0, off + h))
    wsp = lambda off: pl.BlockSpec((taps, HEAD_DIM), lambda bb, h: (0, off + h))
    qb = col_qkv // HEAD_DIM
    return pl.pallas_call(
        kern,
        grid=(b, n_heads),
        in_specs=[seq(qb), seq(qb + n_heads), seq(qb + 2 * n_heads),
                  wsp(0), wsp(n_heads), wsp(2 * n_heads),
                  pl.BlockSpec((1, s, HEAD_DIM), lambda bb, h: (bb, 0, 0)),
                  seq(col_z // HEAD_DIM),
                  pl.BlockSpec((1, HEAD_DIM), lambda bb, h: (0, 0))],
        out_specs=seq(0),
        out_shape=jax.ShapeDtypeStruct((b, s, n_heads * HEAD_DIM), BF16),
        scratch_shapes=[pltpu.VMEM((s, HEAD_DIM), F32)] * 3
                       + [pltpu.VMEM((2, nc, 3 * CHUNK, HEAD_DIM), BF16),
                          pltpu.VMEM((2, nc, 2 * CHUNK, HEAD_DIM), F32),
                          pltpu.VMEM((nc, SUBLANES, HEAD_DIM), F32)]
                       + [pltpu.VMEM((s + 3 * CONV_PAD, HEAD_DIM), F32)] * 3,
        compiler_params=_cparams(("parallel", "arbitrary")),
        name="gated_deltanet",
    )(p, p, p, conv_w, conv_w, conv_w, gates, p, onorm_g.reshape(1, HEAD_DIM))


def _ret_kernel(lg_ref, q_ref, k_ref, v_ref, z_ref, c2_ref, s2_ref, g_ref, y_ref,
                o_scr, q_scr, kzt_scr, *, n_chunks, n_ctx_chunks, group):
    h = pl.program_id(1)
    L = CHUNK
    L2 = 2 * L
    ii = lax.broadcasted_iota(jnp.int32, (L, L), 0)
    jj = lax.broadcasted_iota(jnp.int32, (L, L), 1)
    row_i = lax.broadcasted_iota(jnp.int32, (L, HEAD_DIM), 0).astype(F32)
    row2 = lax.broadcasted_iota(jnp.int32, (L2, HEAD_DIM), 0).astype(F32)
    dist = (ii - jj).astype(F32)
    lgs = (lg_ref[0, h], lg_ref[1, h])
    dsum = (jnp.where(ii >= jj, jnp.exp(jnp.maximum(dist, 0.0) * lgs[0]), 0.0)
            + jnp.where(ii <= jj, jnp.exp(jnp.maximum(-dist, 0.0) * lgs[1]), 0.0))
    xi = (jnp.exp((row_i + 1.0) * lgs[0]), jnp.exp((L - row_i) * lgs[1]))
    zeta2 = jnp.where(row2 < L, jnp.exp((L - 1.0 - row2) * lgs[0]), jnp.exp((row2 - L) * lgs[1]))
    gch = tuple(jnp.exp(jnp.full((1, HEAD_DIM), float(L), F32) * lg) for lg in lgs)
    zeros = jnp.zeros((L, HEAD_DIM), v_ref.dtype)

    def prepare_group(i, _):
        cs = [i * group + g for g in range(group)]
        rows = [pl.ds(pl.multiple_of(c * L, L), L) for c in cs]
        q = _each(lambda r: _rope(q_ref[0, r, :].astype(F32), c2_ref[r, :], s2_ref[r, :]).astype(BF16), rows)
        k = _each(lambda r: _rope(k_ref[0, r, :].astype(F32), c2_ref[r, :], s2_ref[r, :])
                  * (HEAD_DIM ** -0.5), rows)
        sc = _each(lambda qq, kk: _dot_nt(qq, kk) * dsum, q, k)
        intra = _each(lambda x, r: _dot(x, v_ref[0, r, :]), sc, rows)
        kzt = _each(lambda kk: (jnp.concatenate([kk, kk], axis=0) * zeta2).T.astype(BF16), k)
        for g, c in enumerate(cs):
            q_scr[rows[g], :] = q[g]
            o_scr[rows[g], :] = intra[g]
            kzt_scr[c] = kzt[g]
        return 0

    lax.fori_loop(0, n_chunks // group, prepare_group, 0)

    def advance(c, d, state):
        r0 = pl.multiple_of(c * L, L)
        v = v_ref[0, pl.ds(r0, L), :]
        vpad = jnp.concatenate([v, zeros] if d == 0 else [zeros, v], axis=0)
        o_scr[pl.ds(r0, L), :] += xi[d] * jnp.dot(q_scr[pl.ds(r0, L), :], state.astype(BF16),
                                                  preferred_element_type=F32)
        return state * gch[d] + jnp.dot(kzt_scr[c], vpad, preferred_element_type=F32)

    def body(n, carry):
        s_up, s_dn = carry
        s_up = advance(n, 0, s_up)
        s_dn = advance(_backward_chunk(n, n_chunks, n_ctx_chunks), 1, s_dn)
        return s_up, s_dn

    zero = jnp.zeros((HEAD_DIM, HEAD_DIM), F32)
    lax.fori_loop(0, n_chunks, body, (zero, zero), unroll=2)

    rows = group * L

    def epilogue(n, _):
        r0 = pl.multiple_of(n * rows, rows)
        o = o_scr[pl.ds(r0, rows), :]
        mu = jnp.mean(o, axis=-1, keepdims=True)
        var = jnp.mean(jnp.square(o - mu), axis=-1, keepdims=True)
        y = (o - mu) * lax.rsqrt(var + EPS) * g_ref[...]
        z = z_ref[0, pl.ds(r0, rows), :].astype(F32)
        y_ref[0, pl.ds(r0, rows), :] = (y * _silu(z)).astype(y_ref.dtype)
        return 0

    lax.fori_loop(0, n_chunks // group, epilogue, 0)


def _retention(lg, p, c2, s2, gnorm_g, n_ctx, n_heads, col_q, col_k, col_v, col_z):
    b, s, _ = p.shape
    nc = s // CHUNK
    kern = functools.partial(_ret_kernel, n_chunks=nc, n_ctx_chunks=n_ctx // CHUNK, group=_group(nc))
    seq = lambda off: pl.BlockSpec((1, s, HEAD_DIM), lambda bb, h: (bb, 0, off // HEAD_DIM + h))
    tab = pl.BlockSpec((s, HEAD_DIM), lambda bb, h: (0, 0))
    return pl.pallas_call(
        kern,
        grid=(b, n_heads),
        in_specs=[pl.BlockSpec(memory_space=pltpu.SMEM), seq(col_q), seq(col_k), seq(col_v), seq(col_z),
                  tab, tab, pl.BlockSpec((1, HEAD_DIM), lambda bb, h: (0, h))],
        out_specs=seq(0),
        out_shape=jax.ShapeDtypeStruct((b, s, n_heads * HEAD_DIM), BF16),
        scratch_shapes=[pltpu.VMEM((s, HEAD_DIM), F32),
                        pltpu.VMEM((s, HEAD_DIM), BF16),
                        pltpu.VMEM((nc, HEAD_DIM, 2 * CHUNK), BF16)],
        compiler_params=_cparams(("parallel", "arbitrary")),
        name="retention",
    )(lg, p, p, p, p, c2, s2, gnorm_g.reshape(1, n_heads * HEAD_DIM))


def _merge_kernel(ya_ref, yd_ref, yr_ref, wa_ref, wd_ref, wr_ref, ga_ref, gd_ref, gr_ref, o_ref):
    gate = lambda g_ref: _sigmoid(g_ref[0].astype(F32))
    acc = gate(ga_ref) * jnp.dot(ya_ref[0], wa_ref[0], preferred_element_type=F32)
    acc = acc + gate(gd_ref) * jnp.dot(yd_ref[0], wd_ref[0], preferred_element_type=F32)
    acc = acc + gate(gr_ref) * jnp.dot(yr_ref[0], wr_ref[0], preferred_element_type=F32)
    o_ref[0] = acc.astype(o_ref.dtype)


def _merge(ya, yd, yr, w_branch, layer, p, col_gates, tm, tn=COL_TILE):
    b, s, bw = ya.shape
    d = w_branch.shape[2]
    gb = col_gates // tn
    ysp = pl.BlockSpec((1, tm, bw), lambda bb, i, j: (bb, i, 0))
    wsp = lambda br: pl.BlockSpec((1, bw, tn), lambda bb, i, j: (layer * N_BRANCH + br, 0, j))
    gsp = lambda br: pl.BlockSpec((1, tm, tn), lambda bb, i, j: (bb, i, gb + br * (d // tn) + j))
    return pl.pallas_call(
        _merge_kernel,
        grid=(b, s // tm, d // tn),
        in_specs=[ysp, ysp, ysp, wsp(0), wsp(1), wsp(2), gsp(0), gsp(1), gsp(2)],
        out_specs=pl.BlockSpec((1, tm, tn), lambda bb, i, j: (bb, i, j)),
        out_shape=jax.ShapeDtypeStruct((b, s, d), BF16),
        compiler_params=_cparams(("parallel", "parallel", "arbitrary")),
        name="branch_merge",
    )(ya, yd, yr, w_branch, w_branch, w_branch, p, p, p)


def _outproj_kernel(m_ref, w_ref, x_ref, lat_ref, ctx_ref, o_ref, *, n_ctx, tm):
    i = pl.program_id(1)
    row = i * tm + lax.broadcasted_iota(jnp.int32, (tm, 1), 0)
    gate = jnp.where(row < n_ctx, ctx_ref[...], lat_ref[0])
    o_ref[0] = x_ref[0] + gate * jnp.dot(m_ref[0], w_ref[0], preferred_element_type=F32)


def _out_projection(merged, w_out, layer, xs, gate_lat, gate_ctx, n_ctx, tm, tn=COL_TILE):
    b, s, d = xs.shape
    kern = functools.partial(_outproj_kernel, n_ctx=n_ctx, tm=tm)
    return pl.pallas_call(
        kern,
        grid=(b, s // tm, d // tn),
        in_specs=[pl.BlockSpec((1, tm, d), lambda bb, i, j: (bb, i, 0)),
                  pl.BlockSpec((1, d, tn), lambda bb, i, j: (layer, 0, j)),
                  pl.BlockSpec((1, tm, tn), lambda bb, i, j: (bb, i, j)),
                  pl.BlockSpec((1, 1, tn), lambda bb, i, j: (bb, 0, j)),
                  pl.BlockSpec((1, tn), lambda bb, i, j: (0, j))],
        out_specs=pl.BlockSpec((1, tm, tn), lambda bb, i, j: (bb, i, j)),
        out_shape=jax.ShapeDtypeStruct((b, s, d), F32),
        compiler_params=_cparams(("parallel", "parallel", "arbitrary")),
        name="out_projection",
    )(merged, w_out, xs, gate_lat, gate_ctx)


def _cast_kernel(x_ref, o_ref):
    o_ref[...] = x_ref[...].astype(o_ref.dtype)


def _to_bf16(x, rows=512):
    r, c = x.shape
    spec = pl.BlockSpec((rows, c), lambda i: (i, 0))
    return pl.pallas_call(
        _cast_kernel,
        grid=(r // rows,),
        in_specs=[spec],
        out_specs=spec,
        out_shape=jax.ShapeDtypeStruct((r, c), BF16),
        compiler_params=_cparams(("parallel",)),
        name="weight_cast",
    )(x)


def _deinterleave(w, n_heads):
    lead = w.shape[:-1]
    w = w.reshape(*lead, n_heads, HALF, 2)
    return jnp.swapaxes(w, -1, -2).reshape(*lead, n_heads * HEAD_DIM)


class _Cols:
    def __init__(self, d_model):
        aw = d_model // 2
        kvw = aw // ATT_GROUPS
        nh = aw // HEAD_DIM
        self.src = dict(att_q=aw, att_k=kvw, att_v=kvw, att_z=aw, dn_qkv=3 * aw, dn_z=aw,
                        dn_b=2 * nh, dn_a=2 * nh, ret_q=aw, ret_k=aw, ret_v=aw, ret_z=aw,
                        gates=N_BRANCH * d_model)
        self.order = ("att_q", "att_k", "att_v", "att_z", "dn_qkv", "dn_z", "ret_q", "ret_k", "ret_v",
                      "ret_z", "dn_b", "dn_a", "gates")
        self.off = {}
        o = 0
        for name in self.order:
            if name == "gates":
                o = -(-o // COL_TILE) * COL_TILE
            self.off[name] = o
            o += self.src[name]
        self.width = -(-o // COL_TILE) * COL_TILE
        self.n_heads = nh


_W_ZERO, _W_COPY, _W_ROPE, _W_GATES = 0, 1, 2, 3


def _wprep_kernel(src_ref, shift_ref, kind_ref, a_ref, b_ref, o_ref, *, lane_shift, gate_cols):
    jb = pl.program_id(0)
    shifted_src = shift_ref[jb] != 0
    kind = kind_ref[jb]
    a = a_ref[0]
    x = a
    if lane_shift:
        x = jnp.where(shifted_src, jnp.concatenate([a[:, lane_shift:], b_ref[0][:, :lane_shift]], axis=1), a)
    xb = x.astype(BF16)
    rr = lax.broadcasted_iota(jnp.int32, (HEAD_DIM, HEAD_DIM), 0)
    cc = lax.broadcasted_iota(jnp.int32, (HEAD_DIM, HEAD_DIM), 1)
    pmat = jnp.where(rr == jnp.where(cc < HALF, 2 * cc, 2 * (cc - HALF) + 1), 1.0, 0.0).astype(BF16)
    lane = lax.broadcasted_iota(jnp.int32, a.shape, 1)
    y = jnp.where(kind == _W_ROPE, jnp.dot(xb, pmat, preferred_element_type=F32).astype(BF16), xb)
    y = jnp.where(kind == _W_GATES, jnp.where(lane < gate_cols, a, 0.0).astype(BF16), y)
    o_ref[...] = jnp.where(kind == _W_ZERO, jnp.zeros_like(y), y)


def _prep_w_in(w_in, layer, cols):
    _, d, n_src = w_in.shape
    offs, o = {}, 0
    for name, width in cols.src.items():
        offs[name] = o
        o += width
    n_blocks = cols.width // HEAD_DIM
    src = [0] * n_blocks
    shift = [0] * n_blocks
    kind = [_W_ZERO] * n_blocks
    gate_cols = cols.src["dn_b"] + cols.src["dn_a"]
    lane_shift = gate_cols % HEAD_DIM
    assert offs["dn_b"] % HEAD_DIM == 0 and cols.off["dn_b"] % HEAD_DIM == 0 and gate_cols <= HEAD_DIM
    kind[cols.off["dn_b"] // HEAD_DIM] = _W_GATES
    src[cols.off["dn_b"] // HEAD_DIM] = offs["dn_b"] // HEAD_DIM
    for name in cols.order:
        if name in ("dn_b", "dn_a"):
            continue
        assert cols.src[name] % HEAD_DIM == 0 and cols.off[name] % HEAD_DIM == 0
        for k in range(cols.src[name] // HEAD_DIM):
            jb = cols.off[name] // HEAD_DIM + k
            col = offs[name] + k * HEAD_DIM
            assert col % HEAD_DIM in (0, lane_shift)
            src[jb], shift[jb] = col // HEAD_DIM, col % HEAD_DIM
            kind[jb] = _W_ROPE if name in ("att_q", "att_k", "ret_q", "ret_k") else _W_COPY
    last = (n_src - 1) // HEAD_DIM
    kern = functools.partial(_wprep_kernel, lane_shift=lane_shift, gate_cols=gate_cols)
    grid_spec = pltpu.PrefetchScalarGridSpec(
        num_scalar_prefetch=3,
        grid=(n_blocks,),
        in_specs=[pl.BlockSpec((1, d, HEAD_DIM), lambda jb, s_, sh_, k_: (layer, 0, s_[jb])),
                  pl.BlockSpec((1, d, HEAD_DIM), lambda jb, s_, sh_, k_: (layer, 0, jnp.minimum(s_[jb] + 1, last)))],
        out_specs=pl.BlockSpec((d, HEAD_DIM), lambda jb, s_, sh_, k_: (0, jb)),
    )
    return pl.pallas_call(
        kern,
        grid_spec=grid_spec,
        out_shape=jax.ShapeDtypeStruct((d, cols.width), BF16),
        compiler_params=_cparams(("arbitrary",)),
        name="w_in_reorder",
    )(jnp.array(src, jnp.int32), jnp.array(shift, jnp.int32), jnp.array(kind, jnp.int32), w_in, w_in)


def _rope_tables(n_ctx, t):
    rows = t // GRID_W
    row, col = jnp.meshgrid(jnp.arange(rows, dtype=F32), jnp.arange(GRID_W, dtype=F32), indexing="ij")
    n_freq = HEAD_DIM // 4
    inv = ROPE_THETA ** (-jnp.arange(n_freq, dtype=F32) / n_freq)
    ang = jnp.concatenate([row.reshape(-1, 1) * inv, col.reshape(-1, 1) * inv], axis=-1)
    cos, sin = jnp.cos(ang), jnp.sin(ang)
    cos = jnp.concatenate([jnp.ones((n_ctx, HALF), F32), cos], axis=0)
    sin = jnp.concatenate([jnp.zeros((n_ctx, HALF), F32), sin], axis=0)
    return jnp.concatenate([cos, cos], axis=-1), jnp.concatenate([-sin, sin], axis=-1)


def _row_tile(s):
    for tm in (1088, 1024, 544, 512, 256, 128):
        if s % tm == 0:
            return tm
    return s


def kernel(x, c, ctx, c_ctx, norm_g, w_mod, b_mod, w_in, att_qnorm_g, att_knorm_g, dn_conv_w,
           dn_a_log, dn_dt_bias, dn_onorm_g, ret_gnorm_g, w_branch, w_out):
    b, t, d = x.shape
    n_ctx = ctx.shape[1]
    depth = w_in.shape[0]
    s = n_ctx + t
    cols = _Cols(d)
    nh = cols.n_heads
    tm = _row_tile(s)

    xs = jnp.concatenate([ctx, x], axis=1)
    c2, s2 = _rope_tables(n_ctx, t)

    rows = -(-(b + 1) // SUBLANES) * SUBLANES
    cc = jnp.zeros((rows, d), F32).at[:b].set(c).at[b].set(c_ctx)
    mod = _modulation(cc, w_mod, b_mod)

    decay = jnp.exp(jnp.linspace(math.log(1.0 / 32), math.log(1.0 / 512), nh, dtype=F32))
    log_gamma = jnp.log1p(-decay)
    lg = jnp.stack([log_gamma, log_gamma[::-1]])

    bw = w_branch.shape[2]
    wb_all = _to_bf16(w_branch.reshape(depth * N_BRANCH * bw, d)).reshape(depth * N_BRANCH, bw, d)
    wo_all = _to_bf16(w_out.reshape(depth * d, d)).reshape(depth, d, d)

    pad_l = jnp.zeros((2 * nh,), F32)
    pad_r = jnp.zeros((HEAD_DIM - 4 * nh,), F32)
    off = cols.off
    for l in range(depth):
        m3 = mod[l].reshape(rows, 3, d)
        mod_lat, mod_ctx = m3[:b], m3[b]
        w = _prep_w_in(w_in, l, cols)
        p = _in_projection(xs, norm_g[l], mod_lat, mod_ctx, w, n_ctx, tm)

        ya = _attention(p, c2, s2, _deinterleave(att_qnorm_g[l], 1), _deinterleave(att_knorm_g[l], 1),
                        n_ctx, nh, off["att_q"], off["att_k"], off["att_v"], off["att_z"])

        nega_lane = jnp.concatenate([pad_l, -jnp.exp(dn_a_log[l].astype(F32)).reshape(-1), pad_r])
        dtb_lane = jnp.concatenate([pad_l, dn_dt_bias[l].astype(F32).reshape(-1), pad_r])
        gates = _gdn_gates(p, nega_lane.reshape(1, HEAD_DIM), dtb_lane.reshape(1, HEAD_DIM), off["dn_b"], nh)
        yd = _gdn(p, dn_conv_w[l].astype(F32), gates, dn_onorm_g[l], n_ctx, nh, off["dn_qkv"], off["dn_z"])

        yr = _retention(lg, p, c2, s2, ret_gnorm_g[l], n_ctx, nh,
                        off["ret_q"], off["ret_k"], off["ret_v"], off["ret_z"])

        merged = _merge(ya, yd, yr, wb_all, l, p, off["gates"], tm)
        xs = _out_projection(merged, wo_all, l, xs, mod_lat[:, 2:3, :], mod_ctx[2:3, :], n_ctx, tm)

    return xs[:, n_ctx:, :]
```
